```python
import math
import jax, jax.numpy as jnp
from jax import lax
import numpy as np

D_MODEL = 2048
BATCH = 1
SEQ = 8192
DEPTH = 2

HEAD_DIM = 128
FOX_HEADS = 6
GLA_HEADS = 4
DSA_HEADS = 6
FOX_W = FOX_HEADS * HEAD_DIM
GLA_DK = HEAD_DIM
GLA_DV = HEAD_DIM
GLA_W = GLA_HEADS * GLA_DV
DSA_W = DSA_HEADS * HEAD_DIM
MIX_W = FOX_W + GLA_W + DSA_W
Q_BLOCK = 128
GLA_GATE_RANK = 16
GLA_GATE_TAU = 16.0
GLA_CHUNK = 64
IDX_HEADS = 16
IDX_DIM = 64
DSA_MAX_TOPK = 256
PEER_HEADS = 8
PEER_DQ = 256
PEER_DHALF = PEER_DQ // 2
PEER_NKEYS = 128
PEER_EXPERTS = PEER_NKEYS * PEER_NKEYS
PEER_TOPK = 16
PEER_BLOCK = 64
RMS_EPS = 1e-6
N_MOD = 6

IN_SIZES = (
    FOX_W, FOX_W, FOX_W, FOX_HEADS,
    GLA_HEADS * GLA_DK, GLA_HEADS * GLA_DK, GLA_W, GLA_W, GLA_GATE_RANK,
    DSA_W, DSA_W, DSA_W,
    IDX_HEADS * IDX_DIM, IDX_DIM, IDX_HEADS,
)
N_IN = sum(IN_SIZES)

kernel_name = 'hybrid_fox_gla_dsa_peer'


def rmsnorm(x, g):
    xf = x.astype(jnp.float32)
    y = xf * lax.rsqrt(jnp.mean(xf * xf, axis=-1, keepdims=True) + RMS_EPS)
    return (y * g.astype(jnp.float32)).astype(x.dtype)


def split_cols(z):
    parts, o = [], 0
    for n in IN_SIZES:
        parts.append(z[..., o:o + n])
        o += n
    return parts


def fox_attention(q, k, v, f_logit, f_bias):
    B, S, H, dh = q.shape
    logf = jax.nn.log_sigmoid(f_logit.astype(jnp.float32) + f_bias.astype(jnp.float32))
    F = jnp.cumsum(logf, axis=1).transpose(0, 2, 1)
    kh = k.transpose(0, 2, 1, 3)
    vh = v.transpose(0, 2, 1, 3)
    scale = dh ** -0.5
    kpos = jnp.arange(S)

    def block(i):
        start = i * Q_BLOCK
        qb = lax.dynamic_slice_in_dim(q, start, Q_BLOCK, axis=1)
        Fq = lax.dynamic_slice_in_dim(F, start, Q_BLOCK, axis=2)
        s = jnp.einsum('bthd,bhsd->bhts', qb, kh).astype(jnp.float32) * scale
        s = s + Fq[..., :, None] - F[..., None, :]
        qpos = start + jnp.arange(Q_BLOCK)
        causal = kpos[None, :] <= qpos[:, None]
        s = jnp.where(causal, s, -jnp.inf)
        p = jax.nn.softmax(s, axis=-1).astype(v.dtype)
        return jnp.einsum('bhts,bhsd->bthd', p, vh)

    out = lax.map(block, jnp.arange(S // Q_BLOCK))
    return out.transpose(1, 0, 2, 3, 4).reshape(B, S, H * dh)


def gla_attention(q, k, v, log_a):
    B, S, H, dk = q.shape
    dv = v.shape[-1]
    C = GLA_CHUNK
    n = S // C

    def chunks(t):
        return t.astype(jnp.float32).reshape(B, n, C, H, t.shape[-1]).transpose(1, 0, 3, 2, 4)

    qc = chunks(q) * (dk ** -0.5)
    kc = chunks(k)
    vc = chunks(v)
    gc = jnp.cumsum(chunks(log_a), axis=3)
    tri = jnp.tril(jnp.ones((C, C), dtype=bool))

    def step(state, inp):
        qi, ki, vi, gi = inp
        inter = jnp.einsum('bhtd,bhde->bhte', qi * jnp.exp(gi), state)
        diff = gi[:, :, :, None, :] - gi[:, :, None, :, :]
        decay = jnp.exp(jnp.where(tri[:, :, None], diff, -jnp.inf))
        A = jnp.einsum('bhtd,bhsd,bhtsd->bhts', qi, ki, decay)
        intra = jnp.einsum('bhts,bhse->bhte', A, vi)
        glast = gi[:, :, -1:, :]
        state = jnp.exp(glast[:, :, 0, :])[..., None] * state + jnp.einsum(
            'bhsd,bhse->bhde', ki * jnp.exp(glast - gi), vi)
        return state, inter + intra

    state0 = jnp.zeros((B, H, dk, dv), jnp.float32)
    _, out = lax.scan(step, state0, (qc, kc, vc, gc))
    return out.transpose(1, 0, 3, 2, 4).reshape(B, S, H, dv)


def dsa_attention(q, k, v, iq, ik, iw):
    B, S, H, dh = q.shape
    topk = min(DSA_MAX_TOPK, S // 4)
    scale = dh ** -0.5
    w = iw.astype(jnp.float32) * (IDX_HEADS ** -0.5)
    kpos = jnp.arange(S)
    gather = jax.vmap(lambda t, i: t[i])

    def block(i):
        start = i * Q_BLOCK
        qpos = start + jnp.arange(Q_BLOCK)
        iqb = lax.dynamic_slice_in_dim(iq, start, Q_BLOCK, axis=1)
        wb = lax.dynamic_slice_in_dim(w, start, Q_BLOCK, axis=1)
        rel = jax.nn.relu(jnp.einsum('bthd,bsd->bths', iqb, ik).astype(jnp.float32) * (IDX_DIM ** -0.5))
        score = jnp.einsum('bths,bth->bts', rel, wb)
        causal = kpos[None, :] <= qpos[:, None]
        score = jnp.where(causal[None], score, -jnp.inf)
        _, idx = lax.top_k(score, topk)
        valid = idx <= qpos[None, :, None]
        kg = gather(k, idx)
        vg = gather(v, idx)
        qb = lax.dynamic_slice_in_dim(q, start, Q_BLOCK, axis=1)
        s = jnp.einsum('bthd,btkhd->bhtk', qb, kg).astype(jnp.float32) * scale
        s = jnp.where(valid[:, None], s, -jnp.inf)
        p = jax.nn.softmax(s, axis=-1).astype(v.dtype)
        return jnp.einsum('bhtk,btkhd->bthd', p, vg)

    out = lax.map(block, jnp.arange(S // Q_BLOCK))
    return out.transpose(1, 0, 2, 3, 4).reshape(B, S, H * dh)


def peer_ffn(h, wq, k1, k2, u, v):
    B, S, D = h.shape
    q = jnp.einsum('bsd,de->bse', h, wq).reshape(B, S, PEER_HEADS, 2, PEER_DHALF)
    s1 = jnp.einsum('bshd,hnd->bshn', q[..., 0, :], k1).astype(jnp.float32)
    s2 = jnp.einsum('bshd,hnd->bshn', q[..., 1, :], k2).astype(jnp.float32)
    v1, i1 = lax.top_k(s1, PEER_TOPK)
    v2, i2 = lax.top_k(s2, PEER_TOPK)
    cand = (v1[..., :, None] + v2[..., None, :]).reshape(B, S, PEER_HEADS, PEER_TOPK * PEER_TOPK)
    cidx = (i1[..., :, None] * PEER_NKEYS + i2[..., None, :]).reshape(B, S, PEER_HEADS, PEER_TOPK * PEER_TOPK)
    best, pos = lax.top_k(cand, PEER_TOPK)
    eidx = jnp.take_along_axis(cidx, pos, axis=-1)
    gate = jax.nn.softmax(best, axis=-1)

    def block(i):
        start = i * PEER_BLOCK
        hb = lax.dynamic_slice_in_dim(h, start, PEER_BLOCK, axis=1)
        eb = lax.dynamic_slice_in_dim(eidx, start, PEER_BLOCK, axis=1)
        gb = lax.dynamic_slice_in_dim(gate, start, PEER_BLOCK, axis=1)
        ub = u[eb]
        vb = v[eb]
        a = jax.nn.gelu(jnp.einsum('btd,bthkd->bthk', hb, ub).astype(jnp.float32))
        return jnp.einsum('bthk,bthkd->btd', (gb * a).astype(h.dtype), vb)

    out = lax.map(block, jnp.arange(S // PEER_BLOCK))
    return out.transpose(1, 0, 2, 3).reshape(B, S, D)


def setup_inputs(seed: int = 0) -> dict:
    key = jax.random.key(seed)
    ks = jax.random.split(key, 20)
    f32 = jnp.float32
    L, D = DEPTH, D_MODEL
    nrm = lambda k, shape, s: jax.random.normal(k, shape, f32) * s
    return {
        'x': nrm(ks[0], (BATCH, SEQ, D), 1.0),
        'c': nrm(ks[1], (BATCH, D), 1.0),
        'ada_w': nrm(ks[2], (L, D, N_MOD * D), 0.5 * D ** -0.5),
        'ada_b': nrm(ks[3], (L, N_MOD * D), 0.01),
        'norm1_g': 1.0 + nrm(ks[4], (L, D), 0.01),
        'norm2_g': 1.0 + nrm(ks[5], (L, D), 0.01),
        'final_g': 1.0 + nrm(ks[6], (D,), 0.01),
        'w_in': nrm(ks[7], (L, D, N_IN), D ** -0.5),
        'fox_fbias': 3.0 + nrm(ks[8], (L, FOX_HEADS), 0.5),
        'gla_wa2': nrm(ks[9], (L, GLA_GATE_RANK, GLA_HEADS * GLA_DK), GLA_GATE_RANK ** -0.5),
        'gla_ba': nrm(ks[10], (L, GLA_HEADS * GLA_DK), 0.01),
        'gla_norm_g': 1.0 + nrm(ks[11], (L, GLA_DV), 0.01),
        'w_out': nrm(ks[12], (L, MIX_W, D), MIX_W ** -0.5),
        'peer_wq': nrm(ks[13], (L, D, PEER_HEADS * PEER_DQ), D ** -0.5),
        'peer_k1': nrm(ks[14], (L, PEER_HEADS, PEER_NKEYS, PEER_DHALF), PEER_DHALF ** -0.5),
        'peer_k2': nrm(ks[15], (L, PEER_HEADS, PEER_NKEYS, PEER_DHALF), PEER_DHALF ** -0.5),
        'peer_u': nrm(ks[16], (L, PEER_EXPERTS, D), D ** -0.5),
        'peer_v': nrm(ks[17], (L, PEER_EXPERTS, D), PEER_HEADS ** -0.5),
    }


def reference(x, c, ada_w, ada_b, norm1_g, norm2_g, final_g, w_in, fox_fbias, gla_wa2, gla_ba,
              gla_norm_g, w_out, peer_wq, peer_k1, peer_k2, peer_u, peer_v):
    B, S, D = x.shape
    c_act = jax.nn.silu(c)
    for l in range(DEPTH):
        mod = jnp.einsum('bd,de->be', c_act, ada_w[l]) + ada_b[l]
        sh1, sc1, g1, sh2, sc2, g2 = [m[:, None, :] for m in jnp.split(mod, N_MOD, axis=-1)]

        h = rmsnorm(x, norm1_g[l]) * (1.0 + sc1) + sh1
        z = jnp.einsum('bsd,de->bse', h, w_in[l])
        (fq, fk, fv, ff, gq, gk, gv, gr, ga, dq, dk, dv, iq, ik, iw) = split_cols(z)

        fox = fox_attention(fq.reshape(B, S, FOX_HEADS, HEAD_DIM), fk.reshape(B, S, FOX_HEADS, HEAD_DIM),
                            fv.reshape(B, S, FOX_HEADS, HEAD_DIM), ff, fox_fbias[l])

        log_a = jax.nn.log_sigmoid(jnp.einsum('bsr,re->bse', ga, gla_wa2[l]).astype(jnp.float32)
                                   + gla_ba[l].astype(jnp.float32)) / GLA_GATE_TAU
        go = gla_attention(gq.reshape(B, S, GLA_HEADS, GLA_DK), gk.reshape(B, S, GLA_HEADS, GLA_DK),
                           gv.reshape(B, S, GLA_HEADS, GLA_DV), log_a.reshape(B, S, GLA_HEADS, GLA_DK))
        gla = (rmsnorm(go, gla_norm_g[l]).reshape(B, S, GLA_W)
               * jax.nn.silu(gr.astype(jnp.float32))).astype(x.dtype)

        dsa = dsa_attention(dq.reshape(B, S, DSA_HEADS, HEAD_DIM), dk.reshape(B, S, DSA_HEADS, HEAD_DIM),
                            dv.reshape(B, S, DSA_HEADS, HEAD_DIM), iq.reshape(B, S, IDX_HEADS, IDX_DIM), ik, iw)

        mix = jnp.concatenate([fox, gla, dsa], axis=-1)
        x = x + g1 * jnp.einsum('bse,ed->bsd', mix, w_out[l])

        h = rmsnorm(x, norm2_g[l]) * (1.0 + sc2) + sh2
        x = x + g2 * peer_ffn(h, peer_wq[l], peer_k1[l], peer_k2[l], peer_u[l], peer_v[l])
    return rmsnorm(x, final_g)
```

```python
import functools

import jax
import jax.numpy as jnp
from jax import lax
from jax.experimental import pallas as pl
from jax.experimental.pallas import tpu as pltpu

F32 = jnp.float32
BF16 = jnp.bfloat16
I32 = jnp.int32

D_MODEL = 2048
HEAD_DIM = 128
FOX_HEADS = 6
GLA_HEADS = 4
DSA_HEADS = 6
FOX_W = FOX_HEADS * HEAD_DIM
GLA_W = GLA_HEADS * HEAD_DIM
DSA_W = DSA_HEADS * HEAD_DIM
GLA_GATE_RANK = 16
GLA_GATE_TAU = 16.0
GLA_CHUNK = 64
IDX_HEADS = 16
IDX_DIM = 64
DSA_MAX_TOPK = 256
PEER_HEADS = 8
PEER_DQ = 256
PEER_NKEYS = 128
PEER_EXPERTS = PEER_NKEYS * PEER_NKEYS
PEER_TOPK = 16
RMS_EPS = 1e-6
N_MOD = 6

LANE = 128
NEG = -1e30
INT_MIN = -(2 ** 31)
VMEM_LIMIT = 56 * 1024 * 1024

COL_IQ = 0
COL_GQ = 2048
COL_GK = 2560
COL_GV = 3072
COL_GR = 3584
COL_FF = 4096
COL_GA = 4224
COL_IK = 4352
COL_IW = 4480
COL_FQ = 4608
COL_FK = 5376
COL_FV = 6144
COL_DQ = 6912
COL_DK = 7680
COL_DV = 8448
N_IN_PAD = 9216


def _cparams(sem):
    return pltpu.CompilerParams(dimension_semantics=sem, vmem_limit_bytes=VMEM_LIMIT)


def _split3(x):
    hi = x.astype(BF16)
    r1 = x - hi.astype(F32)
    mid = r1.astype(BF16)
    lo = (r1 - mid.astype(F32)).astype(BF16)
    return hi, mid, lo


def _dot(a, b):
    return jnp.dot(a, b, preferred_element_type=F32)


def _dot_nt(a, b):
    return lax.dot_general(a, b, (((1,), (1,)), ((), ())), preferred_element_type=F32)


def _dot_tn(a, b):
    return lax.dot_general(a, b, (((0,), (0,)), ((), ())), preferred_element_type=F32)


def _mod_kernel(c_ref, w_ref, b_ref, o_ref):
    c = c_ref[...]
    ca = jax.nn.silu(c)
    o_ref[0] = jnp.sum(w_ref[0] * ca, axis=0, keepdims=True) + b_ref[0]


def _modulation(c, ada_w, ada_b):
    L, D, E = ada_w.shape
    tn = 1024
    return pl.pallas_call(
        _mod_kernel,
        grid=(L, E // tn),
        in_specs=[pl.BlockSpec((D, 1), lambda l, j: (0, 0)),
                  pl.BlockSpec((1, D, tn), lambda l, j: (l, 0, j)),
                  pl.BlockSpec((1, 1, tn), lambda l, j: (l, 0, j))],
        out_specs=pl.BlockSpec((1, 1, tn), lambda l, j: (l, 0, j)),
        out_shape=jax.ShapeDtypeStruct((L, 1, E), F32),
        compiler_params=_cparams(("arbitrary", "arbitrary")),
        name="adaln_mod",
    )(c.reshape(D, 1), ada_w, ada_b.reshape(L, 1, E))


def _ada_norm(x, g, sc, sh):
    y = x * lax.rsqrt(jnp.mean(x * x, axis=-1, keepdims=True) + RMS_EPS) * g
    return y * (1.0 + sc) + sh


def _inproj_kernel(x_ref, g_ref, sc_ref, sh_ref, w_ref, o_ref, h_scr):
    @pl.when(pl.program_id(1) == 0)
    def _():
        h_scr[...] = _ada_norm(x_ref[...], g_ref[...], sc_ref[...], sh_ref[...]).astype(BF16)

    o_ref[...] = _dot(h_scr[...], w_ref[...])


def _inproj(x, g, sc, sh, w_p):
    S, D = x.shape
    N = w_p.shape[1]
    tm, tn = 512, 512
    vec = pl.BlockSpec((1, D), lambda i, j: (0, 0))
    return pl.pallas_call(
        _inproj_kernel,
        grid=(S // tm, N // tn),
        in_specs=[pl.BlockSpec((tm, D), lambda i, j: (i, 0)), vec, vec, vec,
                  pl.BlockSpec((D, tn), lambda i, j: (0, j))],
        out_specs=pl.BlockSpec((tm, tn), lambda i, j: (i, j)),
        out_shape=jax.ShapeDtypeStruct((S, N), F32),
        scratch_shapes=[pltpu.VMEM((tm, D), BF16)],
        compiler_params=_cparams(("arbitrary", "arbitrary")),
        name="norm_inproj",
    )(x, g, sc, sh, w_p)


def _forget_kernel(ff_ref, fb_ref, fcol_ref, frow_ref, carry):
    @pl.when(pl.program_id(0) == 0)
    def _():
        carry[...] = jnp.zeros_like(carry)

    tm = ff_ref.shape[0]
    logf = jax.nn.log_sigmoid(ff_ref[...] + fb_ref[...])
    r = lax.broadcasted_iota(I32, (tm, tm), 0)
    c = lax.broadcasted_iota(I32, (tm, tm), 1)
    tril = jnp.where(c <= r, 1.0, 0.0).astype(BF16)
    hi, mid, lo = _split3(logf)
    cum = _dot(tril, hi) + _dot(tril, mid) + _dot(tril, lo) + carry[...]
    carry[...] = cum[tm - 1:tm, :]
    fcol_ref[...] = cum
    frow_ref[...] = cum.T[0:8, :]


def _forget_cumsum(z, fbias_pad):
    S = z.shape[0]
    tm = 256
    return pl.pallas_call(
        _forget_kernel,
        grid=(S // tm,),
        in_specs=[pl.BlockSpec((tm, LANE), lambda i: (i, COL_FF // LANE)),
                  pl.BlockSpec((1, LANE), lambda i: (0, 0))],
        out_specs=[pl.BlockSpec((tm, LANE), lambda i: (i, 0)),
                   pl.BlockSpec((8, tm), lambda i: (0, i))],
        out_shape=[jax.ShapeDtypeStruct((S, LANE), F32), jax.ShapeDtypeStruct((8, S), F32)],
        scratch_shapes=[pltpu.VMEM((1, LANE), F32)],
        compiler_params=_cparams(("arbitrary",)),
        name="fox_forget_cumsum",
    )(z, fbias_pad)


def _flash_kernel(*refs, fox, nh, tq, tk):
    if fox:
        q_ref, k_ref, v_ref, fq_ref, fk_ref, o_ref, m_scr, l_scr, acc_scr = refs
    else:
        q_ref, k_ref, v_ref, b_ref, o_ref, m_scr, l_scr, acc_scr = refs
    i = pl.program_id(0)
    j = pl.program_id(1)
    scale = HEAD_DIM ** -0.5

    @pl.when(j == 0)
    def _():
        m_scr[...] = jnp.full_like(m_scr, NEG)
        l_scr[...] = jnp.zeros_like(l_scr)
        acc_scr[...] = jnp.zeros_like(acc_scr)

    @pl.when(j <= i)
    def _():
        if fox:
            rows = i * tq + lax.broadcasted_iota(I32, (tq, tk), 0)
            cols = j * tk + lax.broadcasted_iota(I32, (tq, tk), 1)
            causal = cols <= rows
        else:
            bias = b_ref[...].astype(F32)
        for h in range(nh):
            sl = slice(h * HEAD_DIM, (h + 1) * HEAD_DIM)
            q = (q_ref[:, sl] * scale).astype(BF16)
            k = k_ref[:, sl].astype(BF16)
            s = _dot_nt(q, k)
            if fox:
                s = s + (fq_ref[:, h:h + 1] - fk_ref[h:h + 1, :])
                s = jnp.where(causal, s, NEG)
            else:
                s = s + bias
            m_prev = m_scr[h]
            m_new = jnp.maximum(m_prev, jnp.max(s, axis=1, keepdims=True))
            alpha = jnp.exp(m_prev - m_new)
            p = jnp.exp(s - m_new)
            l_scr[h] = alpha * l_scr[h] + jnp.sum(p, axis=1, keepdims=True)
            m_scr[h] = m_new
            pv = _dot(p.astype(BF16), v_ref[:, sl].astype(BF16))
            acc_scr[:, sl] = alpha * acc_scr[:, sl] + pv

    @pl.when(j == i)
    def _():
        for h in range(nh):
            sl = slice(h * HEAD_DIM, (h + 1) * HEAD_DIM)
            o_ref[:, sl] = acc_scr[:, sl] / l_scr[h]


def _flash(z, col_q, col_k, col_v, nh, *, fcol=None, frow=None, bias=None):
    S = z.shape[0]
    W = nh * HEAD_DIM
    tq = tk = 256
    fox = bias is None
    qspec = pl.BlockSpec((tq, W), lambda i, j: (i, col_q // W))
    kspec = pl.BlockSpec((tk, W), lambda i, j: (jnp.minimum(j, i), col_k // W))
    vspec = pl.BlockSpec((tk, W), lambda i, j: (jnp.minimum(j, i), col_v // W))
    if fox:
        extra = [fcol, frow]
        especs = [pl.BlockSpec((tq, LANE), lambda i, j: (i, 0)),
                  pl.BlockSpec((8, tk), lambda i, j: (0, jnp.minimum(j, i)))]
    else:
        extra = [bias]
        especs = [pl.BlockSpec((tq, tk), lambda i, j: (i, jnp.minimum(j, i)))]
    return pl.pallas_call(
        functools.partial(_flash_kernel, fox=fox, nh=nh, tq=tq, tk=tk),
        grid=(S // tq, S // tk),
        in_specs=[qspec, kspec, vspec] + especs,
        out_specs=pl.BlockSpec((tq, W), lambda i, j: (i, 0)),
        out_shape=jax.ShapeDtypeStruct((S, W), F32),
        scratch_shapes=[pltpu.VMEM((nh, tq, 1), F32), pltpu.VMEM((nh, tq, 1), F32),
                        pltpu.VMEM((tq, W), F32)],
        compiler_params=_cparams(("arbitrary", "arbitrary")),
        name="fox_flash" if fox else "dsa_flash",
    )(z, z, z, *extra)


def _gla_kernel(q_ref, k_ref, v_ref, r_ref, a_ref, wa_ref, ba_ref, gn_ref, o_ref, st_scr):
    @pl.when(pl.program_id(1) == 0)
    def _():
        st_scr[...] = jnp.zeros_like(st_scr)

    C = GLA_CHUNK
    tm = q_ref.shape[0]
    r = lax.broadcasted_iota(I32, (C, C), 0)
    c = lax.broadcasted_iota(I32, (C, C), 1)
    lower = c <= r
    tril = jnp.where(lower, 1.0, 0.0).astype(BF16)

    a_hi, a_mid, _ = _split3(a_ref[...])
    w_hi, w_mid, _ = _split3(wa_ref[...])
    gate_x = _dot(a_hi, w_hi) + (_dot(a_hi, w_mid) + _dot(a_mid, w_hi)) + ba_ref[...]
    log_a = jax.nn.log_sigmoid(gate_x) / GLA_GATE_TAU

    scale = HEAD_DIM ** -0.5
    st = st_scr[...]
    for ci in range(tm // C):
        rows = slice(ci * C, (ci + 1) * C)
        la_hi, la_mid, la_lo = _split3(log_a[rows])
        g = _dot(tril, la_hi) + _dot(tril, la_mid) + _dot(tril, la_lo)
        q = q_ref[rows, :] * scale
        k = k_ref[rows, :]
        v = v_ref[rows, :].astype(BF16)
        g_ref_row = g[C // 2:C // 2 + 1, :]
        g_last = g[C - 1:C, :]
        inter = _dot_nt((q * jnp.exp(g)).astype(BF16), st.astype(BF16))
        qr = (q * jnp.exp(jnp.minimum(g - g_ref_row, 80.0))).astype(BF16)
        kr = (k * jnp.exp(jnp.minimum(g_ref_row - g, 80.0))).astype(BF16)
        att = jnp.where(lower, _dot_nt(qr, kr), 0.0)
        intra = _dot(att.astype(BF16), v)
        kd = (k * jnp.exp(g_last - g)).astype(BF16)
        st = st * jnp.exp(g_last) + _dot_tn(v, kd)
        go = inter + intra
        y = go * lax.rsqrt(jnp.mean(go * go, axis=-1, keepdims=True) + RMS_EPS) * gn_ref[...]
        o_ref[rows, :] = y * jax.nn.silu(r_ref[rows, :])
    st_scr[...] = st


def _gla(z, wa2_pad, ba, gnorm):
    S = z.shape[0]
    tm = 256
    hb = lambda col: (lambda h, i: (i, col // LANE + h))
    return pl.pallas_call(
        _gla_kernel,
        grid=(GLA_HEADS, S // tm),
        in_specs=[pl.BlockSpec((tm, LANE), hb(COL_GQ)), pl.BlockSpec((tm, LANE), hb(COL_GK)),
                  pl.BlockSpec((tm, LANE), hb(COL_GV)), pl.BlockSpec((tm, LANE), hb(COL_GR)),
                  pl.BlockSpec((tm, LANE), lambda h, i: (i, COL_GA // LANE)),
                  pl.BlockSpec((LANE, LANE), lambda h, i: (0, h)),
                  pl.BlockSpec((1, LANE), lambda h, i: (0, h)),
                  pl.BlockSpec((1, LANE), lambda h, i: (0, 0))],
        out_specs=pl.BlockSpec((tm, LANE), lambda h, i: (i, h)),
        out_shape=jax.ShapeDtypeStruct((S, GLA_W), F32),
        scratch_shapes=[pltpu.VMEM((HEAD_DIM, HEAD_DIM), F32)],
        compiler_params=_cparams(("arbitrary", "arbitrary")),
        name="gla_chunked",
    )(z, z, z, z, z, wa2_pad, ba, gnorm)


def _sort_key(x):
    u = pltpu.bitcast(x, I32)
    return u ^ ((u >> 31) & 0x7FFFFFFF)


def _indexer_kernel(iq_ref, ik_ref, iw_ref, o_ref, key_scr, *, tq, topk):
    i = pl.program_id(0)
    S = ik_ref.shape[0]
    nk = S // tq
    w = iw_ref[...] * (IDX_HEADS ** -0.5 * IDX_DIM ** -0.5)
    rows = i * tq + lax.broadcasted_iota(I32, (tq, tq), 0)
    col0 = lax.broadcasted_iota(I32, (tq, tq), 1)

    def score_chunk(jc, carry):
        off = pl.multiple_of(jc * tq, tq)
        ikb = ik_ref[pl.ds(off, tq), :].astype(BF16)
        acc = jnp.zeros((tq, tq), F32)
        for h in range(IDX_HEADS):
            qh = iq_ref[:, h * LANE:(h + 1) * LANE].astype(BF16)
            acc = acc + jnp.maximum(_dot_nt(qh, ikb), 0.0) * w[:, h:h + 1]
        key = jnp.where(off + col0 <= rows, _sort_key(acc), INT_MIN)
        key_scr[:, pl.ds(off, tq)] = key
        return carry

    lax.fori_loop(0, i + 1, score_chunk, 0)

    def bisect(it, thr):
        cand = thr + lax.shift_left(jnp.int32(1), 31 - it)

        def count_chunk(jc, cnt):
            off = pl.multiple_of(jc * tq, tq)
            ge = (key_scr[:, pl.ds(off, tq)] >= cand).astype(I32)
            return cnt + jnp.sum(ge, axis=1, keepdims=True)

        cnt = lax.fori_loop(0, i + 1, count_chunk, jnp.zeros((tq, 1), I32))
        return jnp.where(cnt >= topk, cand, thr)

    thr = lax.fori_loop(0, 32, bisect, jnp.full((tq, 1), INT_MIN, I32))
    thr = jnp.maximum(thr, INT_MIN + 1)

    def write_chunk(jc, carry):
        off = pl.multiple_of(jc * tq, tq)
        keep = key_scr[:, pl.ds(off, tq)] >= thr
        o_ref[:, pl.ds(off, tq)] = jnp.where(keep, 0.0, NEG).astype(BF16)
        return carry

    lax.fori_loop(0, i + 1, write_chunk, 0)

    def fill_chunk(jc, carry):
        off = pl.multiple_of(jc * tq, tq)
        o_ref[:, pl.ds(off, tq)] = jnp.full((tq, tq), NEG, BF16)
        return carry

    lax.fori_loop(i + 1, nk, fill_chunk, 0)


def _dsa_mask(z):
    S = z.shape[0]
    tq = 256
    topk = min(DSA_MAX_TOPK, S // 4)
    return pl.pallas_call(
        functools.partial(_indexer_kernel, tq=tq, topk=topk),
        grid=(S // tq,),
        in_specs=[pl.BlockSpec((tq, IDX_HEADS * LANE), lambda i: (i, COL_IQ // (IDX_HEADS * LANE))),
                  pl.BlockSpec((S, LANE), lambda i: (0, COL_IK // LANE)),
                  pl.BlockSpec((tq, LANE), lambda i: (i, COL_IW // LANE))],
        out_specs=pl.BlockSpec((tq, S), lambda i: (i, 0)),
        out_shape=jax.ShapeDtypeStruct((S, S), BF16),
        scratch_shapes=[pltpu.VMEM((tq, S), I32)],
        compiler_params=_cparams(("arbitrary",)),
        name="dsa_indexer_mask",
    )(z, z, z)


def _outproj_kernel(fox_ref, gla_ref, dsa_ref, w_ref, x_ref, g1_ref, n2_ref, sc_ref, sh_ref,
                    xo_ref, h_ref):
    acc = _dot(fox_ref[...].astype(BF16), w_ref[0:FOX_W, :])
    acc = acc + _dot(gla_ref[...].astype(BF16), w_ref[FOX_W:FOX_W + GLA_W, :])
    acc = acc + _dot(dsa_ref[...].astype(BF16), w_ref[FOX_W + GLA_W:, :])
    xn = x_ref[...] + g1_ref[...] * acc
    xo_ref[...] = xn
    h_ref[...] = _ada_norm(xn, n2_ref[...], sc_ref[...], sh_ref[...]).astype(BF16)


def _outproj(fox, gla, dsa, w_out, x, g1, n2, sc2, sh2):
    S, D = x.shape
    tm = 256
    vec = pl.BlockSpec((1, D), lambda i: (0, 0))
    row = lambda w: pl.BlockSpec((tm, w), lambda i: (i, 0))
    return pl.pallas_call(
        _outproj_kernel,
        grid=(S // tm,),
        in_specs=[row(FOX_W), row(GLA_W), row(DSA_W), pl.BlockSpec((D, D), lambda i: (0, 0)),
                  row(D), vec, vec, vec, vec],
        out_specs=[row(D), row(D)],
        out_shape=[jax.ShapeDtypeStruct((S, D), F32), jax.ShapeDtypeStruct((S, D), BF16)],
        compiler_params=_cparams(("arbitrary",)),
        name="outproj_residual_norm",
    )(fox, gla, dsa, w_out, x, g1, n2, sc2, sh2)


def _top_rows(x, n):
    rows = x.shape[0]
    ridx = lax.broadcasted_iota(I32, x.shape, 0)
    x0 = x
    vals = []
    for _ in range(n):
        m = jnp.max(x, axis=0, keepdims=True)
        first = jnp.min(jnp.where(x == m, ridx, rows), axis=0, keepdims=True)
        x = jnp.where(ridx == first, -jnp.inf, x)
        vals.append(m)
    return jnp.concatenate(vals, axis=0), x != x0


def _peer_score_kernel(h_ref, wq_ref, k1_ref, k2_ref, s1_ref, s2_ref, e1_ref, e2_ref, tau_ref):
    q = _dot(h_ref[...], wq_ref[...])
    half = PEER_DQ // 2
    taus = []
    for h in range(PEER_HEADS):
        q1 = q[:, h * PEER_DQ:h * PEER_DQ + half].astype(BF16)
        q2 = q[:, h * PEER_DQ + half:(h + 1) * PEER_DQ].astype(BF16)
        s1 = _dot_nt(k1_ref[h], q1)
        s2 = _dot_nt(k2_ref[h], q2)
        v1, in1 = _top_rows(s1, PEER_TOPK)
        v2, in2 = _top_rows(s2, PEER_TOPK)
        cand = jnp.concatenate([v1[a:a + 1, :] + v2 for a in range(PEER_TOPK)], axis=0)
        best, _ = _top_rows(cand, PEER_TOPK)
        zsum = jnp.sum(jnp.exp(best - best[0:1, :]), axis=0, keepdims=True)
        s1m = jnp.where(in1, s1, -jnp.inf)
        s2m = jnp.where(in2, s2, -jnp.inf)
        s1_ref[h] = s1m
        s2_ref[h] = s2m
        e1_ref[h] = jnp.exp(s1m - v1[0:1, :])
        e2_ref[h] = jnp.exp(s2m - v2[0:1, :]) / zsum
        taus.append(best[PEER_TOPK - 1:PEER_TOPK, :])
    tau_ref[...] = jnp.concatenate(taus, axis=0)


def _peer_scores(h2, wq, k1, k2):
    S, D = h2.shape
    tm = 256
    big = pl.BlockSpec((PEER_HEADS, PEER_NKEYS, tm), lambda i: (0, 0, i))
    kspec = pl.BlockSpec((PEER_HEADS, PEER_NKEYS, PEER_DQ // 2), lambda i: (0, 0, 0))
    shp = jax.ShapeDtypeStruct((PEER_HEADS, PEER_NKEYS, S), F32)
    return pl.pallas_call(
        _peer_score_kernel,
        grid=(S // tm,),
        in_specs=[pl.BlockSpec((tm, D), lambda i: (i, 0)),
                  pl.BlockSpec((D, PEER_HEADS * PEER_DQ), lambda i: (0, 0)), kspec, kspec],
        out_specs=[big, big, big, big, pl.BlockSpec((PEER_HEADS, tm), lambda i: (0, i))],
        out_shape=[shp, shp, shp, shp, jax.ShapeDtypeStruct((PEER_HEADS, S), F32)],
        compiler_params=_cparams(("arbitrary",)),
        name="peer_scores",
    )(h2, wq, k1, k2)


def _peer_mix_kernel(h_ref, u_ref, v_ref, s1_ref, s2_ref, e1_ref, e2_ref, tau_ref, x_ref, g2_ref,
                     o_ref, acc_scr, *, eb):
    j = pl.program_id(1)

    @pl.when(j == 0)
    def _():
        acc_scr[...] = jnp.zeros_like(acc_scr)

    act = jax.nn.gelu(_dot_nt(u_ref[...], h_ref[...]))
    parts = []
    for cc in range(eb // PEER_NKEYS):
        c = j * (eb // PEER_NKEYS) + cc
        g = None
        for h in range(PEER_HEADS):
            s1c = s1_ref[h, pl.ds(c, 1), :]
            e1c = e1_ref[h, pl.ds(c, 1), :]
            keep = (s1c + s2_ref[h]) >= tau_ref[h:h + 1, :]
            gh = jnp.where(keep, e2_ref[h], 0.0) * e1c
            g = gh if g is None else g + gh
        parts.append((g * act[cc * PEER_NKEYS:(cc + 1) * PEER_NKEYS, :]).astype(BF16))
    wt = jnp.concatenate(parts, axis=0)
    acc_scr[...] += _dot_tn(wt, v_ref[...])

    @pl.when(j == pl.num_programs(1) - 1)
    def _():
        o_ref[...] = x_ref[...] + g2_ref[...] * acc_scr[...]


def _peer_mix(h2, u, v, s1, s2, e1, e2, tau, x, g2):
    S, D = x.shape
    tm, eb = 256, 512
    big = pl.BlockSpec((PEER_HEADS, PEER_NKEYS, tm), lambda i, j: (0, 0, i))
    row = pl.BlockSpec((tm, D), lambda i, j: (i, 0))
    tab = pl.BlockSpec((eb, D), lambda i, j: (j, 0))
    return pl.pallas_call(
        functools.partial(_peer_mix_kernel, eb=eb),
        grid=(S // tm, PEER_EXPERTS // eb),
        in_specs=[row, tab, tab, big, big, big, big,
                  pl.BlockSpec((PEER_HEADS, tm), lambda i, j: (0, i)), row,
                  pl.BlockSpec((1, D), lambda i, j: (0, 0))],
        out_specs=row,
        out_shape=jax.ShapeDtypeStruct((S, D), F32),
        scratch_shapes=[pltpu.VMEM((tm, D), F32)],
        compiler_params=_cparams(("arbitrary", "arbitrary")),
        name="peer_dense_mix",
    )(h2, u, v, s1, s2, e1, e2, tau, x, g2)


def _final_norm_kernel(x_ref, g_ref, o_ref):
    x = x_ref[...]
    o_ref[...] = x * lax.rsqrt(jnp.mean(x * x, axis=-1, keepdims=True) + RMS_EPS) * g_ref[...]


def _final_norm(x, g):
    S, D = x.shape
    tm = 512
    return pl.pallas_call(
        _final_norm_kernel,
        grid=(S // tm,),
        in_specs=[pl.BlockSpec((tm, D), lambda i: (i, 0)), pl.BlockSpec((1, D), lambda i: (0, 0))],
        out_specs=pl.BlockSpec((tm, D), lambda i: (i, 0)),
        out_shape=jax.ShapeDtypeStruct((S, D), F32),
        compiler_params=_cparams(("arbitrary",)),
        name="final_rmsnorm",
    )(x, g)


def _pad_cols(w, width):
    return jnp.pad(w, ((0, 0), (0, width - w.shape[1])))


def _layout_w_in(w):
    D = w.shape[0]
    sizes = (FOX_W, FOX_W, FOX_W, FOX_HEADS, GLA_W, GLA_W, GLA_W, GLA_W, GLA_GATE_RANK,
             DSA_W, DSA_W, DSA_W, IDX_HEADS * IDX_DIM, IDX_DIM, IDX_HEADS)
    parts, o = [], 0
    for n in sizes:
        parts.append(w[:, o:o + n])
        o += n
    fq, fk, fv, ff, gq, gk, gv, gr, ga, dq, dk, dv, iq, ik, iw = parts
    iq = jnp.pad(iq.reshape(D, IDX_HEADS, IDX_DIM), ((0, 0), (0, 0), (0, LANE - IDX_DIM)))
    iq = iq.reshape(D, IDX_HEADS * LANE)
    cols = [iq, gq, gk, gv, gr, _pad_cols(ff, LANE), _pad_cols(ga, LANE), _pad_cols(ik, LANE),
            _pad_cols(iw, LANE), fq, fk, fv, dq, dk, dv]
    return jnp.concatenate(cols, axis=1).astype(BF16)


def kernel(x, c, ada_w, ada_b, norm1_g, norm2_g, final_g, w_in, fox_fbias, gla_wa2, gla_ba,
           gla_norm_g, w_out, peer_wq, peer_k1, peer_k2, peer_u, peer_v):
    B, S, D = x.shape
    assert B == 1 and D == D_MODEL
    L = ada_w.shape[0]
    xs = x.reshape(S, D)
    mod = _modulation(c, ada_w, ada_b)
    for l in range(L):
        sh1, sc1, g1, sh2, sc2, g2 = [mod[l, :, m * D:(m + 1) * D] for m in range(N_MOD)]
        z = _inproj(xs, norm1_g[l].reshape(1, D), sc1, sh1, _layout_w_in(w_in[l]))

        fcol, frow = _forget_cumsum(z, jnp.pad(fox_fbias[l], (0, LANE - FOX_HEADS)).reshape(1, LANE))
        fox = _flash(z, COL_FQ, COL_FK, COL_FV, FOX_HEADS, fcol=fcol, frow=frow)

        wa2_pad = jnp.pad(gla_wa2[l], ((0, LANE - GLA_GATE_RANK), (0, 0)))
        gla = _gla(z, wa2_pad, gla_ba[l].reshape(1, GLA_W), gla_norm_g[l].reshape(1, HEAD_DIM))

        dsa = _flash(z, COL_DQ, COL_DK, COL_DV, DSA_HEADS, bias=_dsa_mask(z))

        xs, h2 = _outproj(fox, gla, dsa, w_out[l].astype(BF16), xs, g1,
                          norm2_g[l].reshape(1, D), sc2, sh2)

        s1, s2, e1, e2, tau = _peer_scores(h2, peer_wq[l].astype(BF16), peer_k1[l].astype(BF16),
                                           peer_k2[l].astype(BF16))
        xs = _peer_mix(h2, peer_u[l].astype(BF16), peer_v[l].astype(BF16), s1, s2, e1, e2, tau,
                       xs, g2)
    return _final_norm(xs, final_g.reshape(1, D)).reshape(B, S, D)
```

```python
import functools

import jax
import jax.numpy as jnp
from jax import lax
from jax.experimental import pallas as pl
from jax.experimental.pallas import tpu as pltpu

F32 = jnp.float32
BF16 = jnp.bfloat16
I32 = jnp.int32

D_MODEL = 2048
HEAD_DIM = 128
FOX_HEADS = 6
GLA_HEADS = 4
DSA_HEADS = 6
FOX_W = FOX_HEADS * HEAD_DIM
GLA_W = GLA_HEADS * HEAD_DIM
DSA_W = DSA_HEADS * HEAD_DIM
GLA_GATE_RANK = 16
GLA_GATE_TAU = 16.0
GLA_CHUNK = 64
IDX_HEADS = 16
IDX_DIM = 64
DSA_MAX_TOPK = 256
PEER_HEADS = 8
PEER_DQ = 256
PEER_NKEYS = 128
PEER_EXPERTS = PEER_NKEYS * PEER_NKEYS
PEER_TOPK = 16
RMS_EPS = 1e-6
N_MOD = 6
N_IN = 3 * FOX_W + FOX_HEADS + 4 * GLA_W + GLA_GATE_RANK + 3 * DSA_W + IDX_HEADS * IDX_DIM + IDX_DIM + IDX_HEADS

LANE = 128
SUBLANE = 8
NEG = -1e30
INT_MIN = -(2 ** 31)
VMEM_LIMIT = 56 * 1024 * 1024
ATTN_SCALE = HEAD_DIM ** -0.5

COL_IQ = 0
COL_GQ = 2048
COL_GK = 2560
COL_GV = 3072
COL_GR = 3584
COL_FF = 4096
COL_GA = 4224
COL_IK = 4352
COL_IW = 4480
COL_FQ = 4608
COL_FK = 5376
COL_FV = 6144
COL_DQ = 6912
COL_DK = 7680
COL_DV = 8448
N_IN_PAD = 9216


def _w_in_groups():
    sizes = (("fq", FOX_W), ("fk", FOX_W), ("fv", FOX_W), ("ff", FOX_HEADS), ("gq", GLA_W),
             ("gk", GLA_W), ("gv", GLA_W), ("gr", GLA_W), ("ga", GLA_GATE_RANK), ("dq", DSA_W),
             ("dk", DSA_W), ("dv", DSA_W), ("iq", IDX_HEADS * IDX_DIM), ("ik", IDX_DIM),
             ("iw", IDX_HEADS))
    dst = dict(fq=COL_FQ, fk=COL_FK, fv=COL_FV, ff=COL_FF, gq=COL_GQ, gk=COL_GK, gv=COL_GV,
               gr=COL_GR, ga=COL_GA, dq=COL_DQ, dk=COL_DK, dv=COL_DV, ik=COL_IK, iw=COL_IW)
    scale = dict(fq=ATTN_SCALE, dq=ATTN_SCALE, gq=ATTN_SCALE)
    groups, o = [], 0
    for name, n in sizes:
        if name == "iq":
            for h in range(IDX_HEADS):
                groups.append((o + h * IDX_DIM, IDX_DIM, COL_IQ + h * LANE, 1.0))
        else:
            groups.append((o, n, dst[name], scale.get(name, 1.0)))
        o += n
    assert o == N_IN
    return tuple(groups)


def _cparams(sem):
    return pltpu.CompilerParams(dimension_semantics=sem, vmem_limit_bytes=VMEM_LIMIT)


def _split3(x):
    hi = x.astype(BF16)
    r1 = x - hi.astype(F32)
    mid = r1.astype(BF16)
    lo = (r1 - mid.astype(F32)).astype(BF16)
    return hi, mid, lo


def _dot(a, b):
    return jnp.dot(a, b, preferred_element_type=F32)


def _dot_nt(a, b):
    return lax.dot_general(a, b, (((1,), (1,)), ((), ())), preferred_element_type=F32)


def _dot_tn(a, b):
    return lax.dot_general(a, b, (((0,), (0,)), ((), ())), preferred_element_type=F32)


def _mod_kernel(c_ref, w_ref, b_ref, o_ref):
    c = c_ref[...]
    ca = jax.nn.silu(c)
    o_ref[0] = jnp.sum(w_ref[0] * ca, axis=0, keepdims=True) + b_ref[0]


def _modulation(c, ada_w, ada_b):
    L, D, E = ada_w.shape
    tn = 1024
    return pl.pallas_call(
        _mod_kernel,
        grid=(L, E // tn),
        in_specs=[pl.BlockSpec((D, 1), lambda l, j: (0, 0)),
                  pl.BlockSpec((1, D, tn), lambda l, j: (l, 0, j)),
                  pl.BlockSpec((1, 1, tn), lambda l, j: (l, 0, j))],
        out_specs=pl.BlockSpec((1, 1, tn), lambda l, j: (l, 0, j)),
        out_shape=jax.ShapeDtypeStruct((L, 1, E), F32),
        compiler_params=_cparams(("arbitrary", "arbitrary")),
        name="adaln_mod",
    )(c.reshape(D, 1), ada_w, ada_b.reshape(L, 1, E))


def _relayout_kernel(w_ref, o_ref):
    o_ref[...] = jnp.zeros_like(o_ref)
    for src, n, dst, scale in _w_in_groups():
        w = w_ref[:, src:src + n]
        if scale != 1.0:
            w = w * scale
        o_ref[:, dst:dst + n] = w.astype(BF16)


def _layout_w_in(w_in, l):
    _, D, N = w_in.shape
    tr = 256
    return pl.pallas_call(
        _relayout_kernel,
        grid=(D // tr,),
        in_specs=[pl.BlockSpec((None, tr, N), lambda i: (l, i, 0))],
        out_specs=pl.BlockSpec((tr, N_IN_PAD), lambda i: (i, 0)),
        out_shape=jax.ShapeDtypeStruct((D, N_IN_PAD), BF16),
        compiler_params=_cparams(("arbitrary",)),
        name="w_in_relayout",
    )(w_in)


def _transpose_cast_kernel(x_ref, o_ref):
    o_ref[...] = x_ref[...].T.astype(BF16)


def _transpose_cast(x, layer, rows, cols, col0, tr, tc, name):
    if layer is None:
        spec = pl.BlockSpec((tr, tc), lambda i, j: (i, col0 // tc + j))
    else:
        spec = pl.BlockSpec((None, tr, tc), lambda i, j: (layer, i, col0 // tc + j))
    return pl.pallas_call(
        _transpose_cast_kernel,
        grid=(rows // tr, cols // tc),
        in_specs=[spec],
        out_specs=pl.BlockSpec((tc, tr), lambda i, j: (j, i)),
        out_shape=jax.ShapeDtypeStruct((cols, rows), BF16),
        compiler_params=_cparams(("arbitrary", "arbitrary")),
        name=name,
    )(x)


def _ada_norm(x, g, sc, sh):
    y = x * lax.rsqrt(jnp.mean(x * x, axis=-1, keepdims=True) + RMS_EPS) * g
    return y * (1.0 + sc) + sh


def _inproj_kernel(x_ref, g_ref, sc_ref, sh_ref, w_ref, o_ref, ob_ref, h_scr):
    @pl.when(pl.program_id(1) == 0)
    def _():
        h_scr[...] = _ada_norm(x_ref[...], g_ref[...], sc_ref[...], sh_ref[...]).astype(BF16)

    z = _dot(h_scr[...], w_ref[...])
    o_ref[...] = z
    ob_ref[...] = z.astype(BF16)


def _inproj(x, g, sc, sh, w_p):
    S, D = x.shape
    N = w_p.shape[1]
    tm, tn = 512, 512
    vec = pl.BlockSpec((1, D), lambda i, j: (0, 0))
    out = pl.BlockSpec((tm, tn), lambda i, j: (i, j))
    return pl.pallas_call(
        _inproj_kernel,
        grid=(S // tm, N // tn),
        in_specs=[pl.BlockSpec((tm, D), lambda i, j: (i, 0)), vec, vec, vec,
                  pl.BlockSpec((D, tn), lambda i, j: (0, j))],
        out_specs=[out, out],
        out_shape=[jax.ShapeDtypeStruct((S, N), F32), jax.ShapeDtypeStruct((S, N), BF16)],
        scratch_shapes=[pltpu.VMEM((tm, D), BF16)],
        compiler_params=_cparams(("arbitrary", "arbitrary")),
        name="norm_inproj",
    )(x, g, sc, sh, w_p)


def _forget_kernel(ff_ref, fb_ref, fcol_ref, frow_ref, carry):
    @pl.when(pl.program_id(0) == 0)
    def _():
        carry[...] = jnp.zeros_like(carry)

    tm = ff_ref.shape[0]
    logf = jax.nn.log_sigmoid(ff_ref[...] + fb_ref[...])
    r = lax.broadcasted_iota(I32, (tm, tm), 0)
    c = lax.broadcasted_iota(I32, (tm, tm), 1)
    tril = jnp.where(c <= r, 1.0, 0.0).astype(BF16)
    hi, mid, lo = _split3(logf)
    cum = _dot(tril, hi) + _dot(tril, mid) + _dot(tril, lo) + carry[...]
    carry[...] = cum[tm - 1:tm, :]
    fcol_ref[...] = cum
    frow_ref[...] = cum.T[0:SUBLANE, :]


def _forget_cumsum(z, fbias_pad):
    S = z.shape[0]
    tm = 256
    return pl.pallas_call(
        _forget_kernel,
        grid=(S // tm,),
        in_specs=[pl.BlockSpec((tm, LANE), lambda i: (i, COL_FF // LANE)),
                  pl.BlockSpec((1, LANE), lambda i: (0, 0))],
        out_specs=[pl.BlockSpec((tm, LANE), lambda i: (i, 0)),
                   pl.BlockSpec((SUBLANE, tm), lambda i: (0, i))],
        out_shape=[jax.ShapeDtypeStruct((S, LANE), F32), jax.ShapeDtypeStruct((SUBLANE, S), F32)],
        scratch_shapes=[pltpu.VMEM((1, LANE), F32)],
        compiler_params=_cparams(("arbitrary",)),
        name="fox_forget_cumsum",
    )(z, fbias_pad)


def _flash_kernel(*refs, fox, nh, tq, tk):
    if fox:
        q_ref, k_ref, vt_ref, fq_ref, fk_ref, o_ref, m_scr, l_scr, acc_scr, s_scr, p_scr = refs
    else:
        q_ref, k_ref, vt_ref, b_ref, o_ref, m_scr, l_scr, acc_scr, s_scr, p_scr = refs
    i = pl.program_id(0)
    j = pl.program_id(1)

    @pl.when(j == 0)
    def _():
        m_scr[...] = jnp.full_like(m_scr, NEG)
        l_scr[...] = jnp.zeros_like(l_scr)
        acc_scr[...] = jnp.zeros_like(acc_scr)

    def step(diagonal):
        for h in range(nh):
            sl = slice(h * HEAD_DIM, (h + 1) * HEAD_DIM)
            s_scr[h] = _dot_nt(k_ref[:, sl], q_ref[:, sl])
        if fox:
            if diagonal:
                keys = lax.broadcasted_iota(I32, (tk, tq), 0)
                qrys = lax.broadcasted_iota(I32, (tk, tq), 1)
                causal = keys <= qrys
        else:
            bias = b_ref[...].astype(F32)
        m_all = m_scr[...]
        l_all = l_scr[...]
        m_rows, l_rows, alphas = [], [], []
        for h in range(nh):
            s = s_scr[h]
            if fox:
                s = s + (fq_ref[h:h + 1, :] - fk_ref[:, h:h + 1])
                if diagonal:
                    s = jnp.where(causal, s, NEG)
            else:
                s = s + bias
            m_prev = m_all[h:h + 1, :]
            m_new = jnp.maximum(m_prev, jnp.max(s, axis=0, keepdims=True))
            alpha = jnp.exp(m_prev - m_new)
            p = jnp.exp(s - m_new)
            l_rows.append(alpha * l_all[h:h + 1, :] + jnp.sum(p, axis=0, keepdims=True))
            m_rows.append(m_new)
            alphas.append(alpha)
            p_scr[h] = p.astype(BF16)
        m_scr[...] = jnp.concatenate(m_rows + [m_all[nh:, :]], axis=0)
        l_scr[...] = jnp.concatenate(l_rows + [l_all[nh:, :]], axis=0)
        for h in range(nh):
            sl = slice(h * HEAD_DIM, (h + 1) * HEAD_DIM)
            acc_scr[sl, :] = alphas[h] * acc_scr[sl, :] + _dot(vt_ref[sl, :], p_scr[h])

    if fox:
        pl.when(j < i)(lambda: step(False))
        pl.when(j == i)(lambda: step(True))
    else:
        pl.when(j <= i)(lambda: step(False))

    @pl.when(j == i)
    def _():
        for h in range(nh):
            sl = slice(h * HEAD_DIM, (h + 1) * HEAD_DIM)
            o_ref[:, sl] = (acc_scr[sl, :] / l_scr[h:h + 1, :]).T


def _flash(zb, vt, col_q, col_k, nh, *, fcol=None, frow=None, bias_t=None):
    S = zb.shape[0]
    W = nh * HEAD_DIM
    tq = tk = 256
    fox = bias_t is None
    kj = lambda i, j: jnp.minimum(j, i)
    qspec = pl.BlockSpec((tq, W), lambda i, j: (i, col_q // W))
    kspec = pl.BlockSpec((tk, W), lambda i, j: (kj(i, j), col_k // W))
    vspec = pl.BlockSpec((W, tk), lambda i, j: (0, kj(i, j)))
    if fox:
        extra = [frow, fcol]
        especs = [pl.BlockSpec((SUBLANE, tq), lambda i, j: (0, i)),
                  pl.BlockSpec((tk, LANE), lambda i, j: (kj(i, j), 0))]
    else:
        extra = [bias_t]
        especs = [pl.BlockSpec((tk, tq), lambda i, j: (kj(i, j), i))]
    return pl.pallas_call(
        functools.partial(_flash_kernel, fox=fox, nh=nh, tq=tq, tk=tk),
        grid=(S // tq, S // tk),
        in_specs=[qspec, kspec, vspec] + especs,
        out_specs=pl.BlockSpec((tq, W), lambda i, j: (i, 0)),
        out_shape=jax.ShapeDtypeStruct((S, W), F32),
        scratch_shapes=[pltpu.VMEM((SUBLANE, tq), F32), pltpu.VMEM((SUBLANE, tq), F32),
                        pltpu.VMEM((W, tq), F32), pltpu.VMEM((nh, tk, tq), F32),
                        pltpu.VMEM((nh, tk, tq), BF16)],
        compiler_params=_cparams(("arbitrary", "arbitrary")),
        name="fox_flash" if fox else "dsa_flash",
    )(zb, zb, vt, *extra)


def _fox_attention(z, zb, fcol, frow):
    vt = _transpose_cast(z, None, z.shape[0], FOX_W, COL_FV, 256, 256, "fox_v_transpose")
    return _flash(zb, vt, COL_FQ, COL_FK, FOX_HEADS, fcol=fcol, frow=frow)


def _dsa_attention(z, zb, bias_t):
    vt = _transpose_cast(z, None, z.shape[0], DSA_W, COL_DV, 256, 256, "dsa_v_transpose")
    return _flash(zb, vt, COL_DQ, COL_DK, DSA_HEADS, bias_t=bias_t)


def _gla_kernel(q_ref, k_ref, v_ref, r_ref, a_ref, wa_ref, ba_ref, gn_ref, o_ref, st_scr):
    @pl.when(pl.program_id(1) == 0)
    def _():
        st_scr[...] = jnp.zeros_like(st_scr)

    C = GLA_CHUNK
    tm = q_ref.shape[0]
    r = lax.broadcasted_iota(I32, (C, C), 0)
    c = lax.broadcasted_iota(I32, (C, C), 1)
    lower = c <= r
    tril = jnp.where(lower, 1.0, 0.0).astype(BF16)

    a_hi, a_mid, _ = _split3(a_ref[...])
    w_hi, w_mid, _ = _split3(wa_ref[...])
    gate_x = _dot(a_hi, w_hi) + (_dot(a_hi, w_mid) + _dot(a_mid, w_hi)) + ba_ref[...]
    log_a = jax.nn.log_sigmoid(gate_x) / GLA_GATE_TAU

    st = st_scr[...]
    for ci in range(tm // C):
        rows = slice(ci * C, (ci + 1) * C)
        la_hi, la_mid, la_lo = _split3(log_a[rows])
        g = _dot(tril, la_hi) + _dot(tril, la_mid) + _dot(tril, la_lo)
        q = q_ref[rows, :]
        k = k_ref[rows, :]
        v = v_ref[rows, :].astype(BF16)
        g_mid = g[C // 2:C // 2 + 1, :]
        g_last = g[C - 1:C, :]
        inter = _dot_nt((q * jnp.exp(g)).astype(BF16), st.astype(BF16))
        qr = (q * jnp.exp(jnp.minimum(g - g_mid, 80.0))).astype(BF16)
        kr = (k * jnp.exp(jnp.minimum(g_mid - g, 80.0))).astype(BF16)
        att = jnp.where(lower, _dot_nt(qr, kr), 0.0)
        intra = _dot(att.astype(BF16), v)
        kd = (k * jnp.exp(g_last - g)).astype(BF16)
        st = st * jnp.exp(g_last) + _dot_tn(v, kd)
        go = inter + intra
        y = go * lax.rsqrt(jnp.mean(go * go, axis=-1, keepdims=True) + RMS_EPS) * gn_ref[...]
        o_ref[rows, :] = y * jax.nn.silu(r_ref[rows, :])
    st_scr[...] = st


def _gla(z, wa2_pad, ba, gnorm):
    S = z.shape[0]
    tm = 256
    hb = lambda col: (lambda h, i: (i, col // LANE + h))
    return pl.pallas_call(
        _gla_kernel,
        grid=(GLA_HEADS, S // tm),
        in_specs=[pl.BlockSpec((tm, LANE), hb(COL_GQ)), pl.BlockSpec((tm, LANE), hb(COL_GK)),
                  pl.BlockSpec((tm, LANE), hb(COL_GV)), pl.BlockSpec((tm, LANE), hb(COL_GR)),
                  pl.BlockSpec((tm, LANE), lambda h, i: (i, COL_GA // LANE)),
                  pl.BlockSpec((LANE, LANE), lambda h, i: (0, h)),
                  pl.BlockSpec((1, LANE), lambda h, i: (0, h)),
                  pl.BlockSpec((1, LANE), lambda h, i: (0, 0))],
        out_specs=pl.BlockSpec((tm, LANE), lambda h, i: (i, h)),
        out_shape=jax.ShapeDtypeStruct((S, GLA_W), F32),
        scratch_shapes=[pltpu.VMEM((HEAD_DIM, HEAD_DIM), F32)],
        compiler_params=_cparams(("arbitrary", "arbitrary")),
        name="gla_chunked",
    )(z, z, z, z, z, wa2_pad, ba, gnorm)


def _sort_key(x):
    u = pltpu.bitcast(x, I32)
    return u ^ ((u >> 31) & 0x7FFFFFFF)


def _indexer_kernel(iq_ref, ik_ref, iw_ref, o_ref, key_scr, *, tq, topk):
    i = pl.program_id(0)
    S = ik_ref.shape[0]
    nk = S // tq
    wt = iw_ref[...].T[0:IDX_HEADS, :] * (IDX_HEADS ** -0.5 * IDX_DIM ** -0.5)
    qrys = i * tq + lax.broadcasted_iota(I32, (tq, tq), 1)
    key0 = lax.broadcasted_iota(I32, (tq, tq), 0)

    def score_chunk(jc, carry):
        off = pl.multiple_of(jc * tq, tq)
        ikb = ik_ref[pl.ds(off, tq), :]
        acc = jnp.zeros((tq, tq), F32)
        for h in range(IDX_HEADS):
            qh = iq_ref[:, h * LANE:(h + 1) * LANE]
            acc = acc + jnp.maximum(_dot_nt(ikb, qh), 0.0) * wt[h:h + 1, :]
        key_scr[pl.ds(off, tq), :] = jnp.where(off + key0 <= qrys, _sort_key(acc), INT_MIN)
        return carry

    lax.fori_loop(0, i + 1, score_chunk, 0)

    def bisect(it, thr):
        cand = thr + lax.shift_left(jnp.int32(1), 31 - it)

        def count_chunk(jc, cnt):
            off = pl.multiple_of(jc * tq, tq)
            ge = jnp.where(key_scr[pl.ds(off, tq), :] >= cand, 1, 0)
            return cnt + jnp.sum(ge.reshape(tq // SUBLANE, SUBLANE, tq), axis=0)

        cnt = lax.fori_loop(0, i + 1, count_chunk, jnp.zeros((SUBLANE, tq), I32))
        return jnp.where(jnp.sum(cnt, axis=0, keepdims=True) >= topk, cand, thr)

    thr = lax.fori_loop(0, 32, bisect, jnp.full((1, tq), INT_MIN, I32))
    thr = jnp.maximum(thr, INT_MIN + 1)

    def write_chunk(jc, carry):
        off = pl.multiple_of(jc * tq, tq)
        keep = key_scr[pl.ds(off, tq), :] >= thr
        o_ref[pl.ds(off, tq), :] = jnp.where(keep, 0.0, NEG).astype(BF16)
        return carry

    lax.fori_loop(0, i + 1, write_chunk, 0)

    def fill_chunk(jc, carry):
        off = pl.multiple_of(jc * tq, tq)
        o_ref[pl.ds(off, tq), :] = jnp.full((tq, tq), NEG, BF16)
        return carry

    lax.fori_loop(i + 1, nk, fill_chunk, 0)


def _dsa_mask(z, zb):
    S = z.shape[0]
    tq = 256
    topk = min(DSA_MAX_TOPK, S // 4)
    return pl.pallas_call(
        functools.partial(_indexer_kernel, tq=tq, topk=topk),
        grid=(S // tq,),
        in_specs=[pl.BlockSpec((tq, IDX_HEADS * LANE), lambda i: (i, COL_IQ // (IDX_HEADS * LANE))),
                  pl.BlockSpec((S, LANE), lambda i: (0, COL_IK // LANE)),
                  pl.BlockSpec((tq, LANE), lambda i: (i, COL_IW // LANE))],
        out_specs=pl.BlockSpec((S, tq), lambda i: (0, i)),
        out_shape=jax.ShapeDtypeStruct((S, S), BF16),
        scratch_shapes=[pltpu.VMEM((S, tq), I32)],
        compiler_params=_cparams(("arbitrary",)),
        name="dsa_indexer_mask",
    )(zb, zb, z)


def _outproj_kernel(fox_ref, gla_ref, dsa_ref, w_ref, x_ref, g1_ref, n2_ref, sc_ref, sh_ref,
                    xo_ref, h_ref):
    acc = _dot(fox_ref[...].astype(BF16), w_ref[0:FOX_W, :])
    acc = acc + _dot(gla_ref[...].astype(BF16), w_ref[FOX_W:FOX_W + GLA_W, :])
    acc = acc + _dot(dsa_ref[...].astype(BF16), w_ref[FOX_W + GLA_W:, :])
    xn = x_ref[...] + g1_ref[...] * acc
    xo_ref[...] = xn
    h_ref[...] = _ada_norm(xn, n2_ref[...], sc_ref[...], sh_ref[...]).astype(BF16)


def _outproj(fox, gla, dsa, w_out, x, g1, n2, sc2, sh2):
    S, D = x.shape
    tm = 256
    vec = pl.BlockSpec((1, D), lambda i: (0, 0))
    row = lambda w: pl.BlockSpec((tm, w), lambda i: (i, 0))
    return pl.pallas_call(
        _outproj_kernel,
        grid=(S // tm,),
        in_specs=[row(FOX_W), row(GLA_W), row(DSA_W), pl.BlockSpec((D, D), lambda i: (0, 0)),
                  row(D), vec, vec, vec, vec],
        out_specs=[row(D), row(D)],
        out_shape=[jax.ShapeDtypeStruct((S, D), F32), jax.ShapeDtypeStruct((S, D), BF16)],
        compiler_params=_cparams(("arbitrary",)),
        name="outproj_residual_norm",
    )(fox, gla, dsa, w_out, x, g1, n2, sc2, sh2)


def _top_rows(x, n):
    rows = x.shape[0]
    ridx = lax.broadcasted_iota(I32, x.shape, 0)
    rank = jnp.full(x.shape, float(n), F32)
    vals = []
    for k in range(n):
        m = jnp.max(x, axis=0, keepdims=True)
        first = jnp.min(jnp.where(x == m, ridx, rows), axis=0, keepdims=True)
        sel = ridx == first
        x = jnp.where(sel, -jnp.inf, x)
        rank = jnp.where(sel, float(k), rank)
        vals.append(m)
    return jnp.concatenate(vals, axis=0), rank


def _peer_score_kernel(h_ref, wq_ref, k1_ref, k2_ref, n_ref, e1_ref, r2_ref, e2_ref, q_scr):
    q = _dot(h_ref[...], wq_ref[...])
    for h in range(PEER_HEADS):
        q_scr[h] = q[:, h * PEER_DQ:(h + 1) * PEER_DQ].astype(BF16)
    half = PEER_DQ // 2
    K = PEER_TOPK

    for h in range(PEER_HEADS):
        qh = q_scr[h]
        s1 = _dot_nt(k1_ref[h], qh[:, :half])
        s2 = _dot_nt(k2_ref[h], qh[:, half:])
        v1, r1 = _top_rows(s1, K)
        v2, r2 = _top_rows(s2, K)
        cand = jnp.concatenate([v1[a:a + 1, :] + v2 for a in range(K)], axis=0)
        best, _ = _top_rows(cand, K)
        tau = best[K - 1:K, :]
        zsum = jnp.sum(jnp.exp(best - best[0:1, :]), axis=0, keepdims=True)
        n = jnp.zeros(s1.shape, F32)
        for a in range(K):
            keep_a = cand[a * K:(a + 1) * K, :] >= tau
            n_a = jnp.sum(jnp.where(keep_a, 1.0, 0.0), axis=0, keepdims=True)
            n = jnp.where(r1 == float(a), n_a, n)
        n_ref[h] = n
        e1_ref[h] = jnp.where(r1 < float(K), jnp.exp(s1 - v1[0:1, :]), 0.0)
        r2_ref[h] = r2.astype(BF16)
        e2_ref[h] = (jnp.where(r2 < float(K), jnp.exp(s2 - v2[0:1, :]), 0.0) / zsum).astype(BF16)


def _peer_scores(h2, wq, k1, k2):
    S, D = h2.shape
    tm = 256
    big = pl.BlockSpec((PEER_HEADS, PEER_NKEYS, tm), lambda i: (0, 0, i))
    kspec = pl.BlockSpec((PEER_HEADS, PEER_NKEYS, PEER_DQ // 2), lambda i: (0, 0, 0))
    f32s = jax.ShapeDtypeStruct((PEER_HEADS, PEER_NKEYS, S), F32)
    b16s = jax.ShapeDtypeStruct((PEER_HEADS, PEER_NKEYS, S), BF16)
    return pl.pallas_call(
        _peer_score_kernel,
        grid=(S // tm,),
        in_specs=[pl.BlockSpec((tm, D), lambda i: (i, 0)),
                  pl.BlockSpec((D, PEER_HEADS * PEER_DQ), lambda i: (0, 0)), kspec, kspec],
        out_specs=[big, big, big, big],
        out_shape=[f32s, f32s, b16s, b16s],
        scratch_shapes=[pltpu.VMEM((PEER_HEADS, tm, PEER_DQ), BF16)],
        compiler_params=_cparams(("arbitrary",)),
        name="peer_scores",
    )(h2, wq, k1, k2)


def _peer_mix_kernel(h_ref, u_ref, vt_ref, n_ref, e1_ref, r2_ref, e2_ref, x_ref, g2_ref,
                     o_ref, acc_scr, p_scr, w_scr, *, eb):
    j = pl.program_id(1)
    ng = eb // PEER_NKEYS

    @pl.when(j == 0)
    def _():
        acc_scr[...] = jnp.zeros_like(acc_scr)

    p_scr[...] = _dot_nt(u_ref[...], h_ref[...])
    zero = jnp.zeros((), BF16)
    for cc in range(ng):
        rows = slice(cc * PEER_NKEYS, (cc + 1) * PEER_NKEYS)
        c = j * ng + cc
        g = None
        for h in range(PEER_HEADS):
            keep = r2_ref[h] < n_ref[h, pl.ds(c, 1), :].astype(BF16)
            gh = jnp.where(keep, e2_ref[h], zero) * e1_ref[h, pl.ds(c, 1), :].astype(BF16)
            g = gh if g is None else g + gh
        w_scr[rows, :] = g * jax.nn.gelu(p_scr[rows, :]).astype(BF16)
    acc_scr[...] += _dot(vt_ref[...], w_scr[...])

    @pl.when(j == pl.num_programs(1) - 1)
    def _():
        o_ref[...] = x_ref[...] + g2_ref[...] * acc_scr[...].T


def _peer_mix(h2, u, vt, n, e1, r2, e2, x, g2):
    S, D = x.shape
    tm, eb = 512, 512
    big = pl.BlockSpec((PEER_HEADS, PEER_NKEYS, tm), lambda i, j: (0, 0, i))
    row = pl.BlockSpec((tm, D), lambda i, j: (i, 0))
    return pl.pallas_call(
        functools.partial(_peer_mix_kernel, eb=eb),
        grid=(S // tm, PEER_EXPERTS // eb),
        in_specs=[row, pl.BlockSpec((eb, D), lambda i, j: (j, 0)),
                  pl.BlockSpec((D, eb), lambda i, j: (0, j)), big, big, big, big, row,
                  pl.BlockSpec((1, D), lambda i, j: (0, 0))],
        out_specs=row,
        out_shape=jax.ShapeDtypeStruct((S, D), F32),
        scratch_shapes=[pltpu.VMEM((D, tm), F32), pltpu.VMEM((eb, tm), F32),
                        pltpu.VMEM((eb, tm), BF16)],
        compiler_params=_cparams(("arbitrary", "arbitrary")),
        name="peer_dense_mix",
    )(h2, u, vt, n, e1, r2, e2, x, g2)


def _final_norm_kernel(x_ref, g_ref, o_ref):
    x = x_ref[...]
    o_ref[...] = x * lax.rsqrt(jnp.mean(x * x, axis=-1, keepdims=True) + RMS_EPS) * g_ref[...]


def _final_norm(x, g):
    S, D = x.shape
    tm = 512
    return pl.pallas_call(
        _final_norm_kernel,
        grid=(S // tm,),
        in_specs=[pl.BlockSpec((tm, D), lambda i: (i, 0)), pl.BlockSpec((1, D), lambda i: (0, 0))],
        out_specs=pl.BlockSpec((tm, D), lambda i: (i, 0)),
        out_shape=jax.ShapeDtypeStruct((S, D), F32),
        compiler_params=_cparams(("arbitrary",)),
        name="final_rmsnorm",
    )(x, g)


def kernel(x, c, ada_w, ada_b, norm1_g, norm2_g, final_g, w_in, fox_fbias, gla_wa2, gla_ba,
           gla_norm_g, w_out, peer_wq, peer_k1, peer_k2, peer_u, peer_v):
    B, S, D = x.shape
    assert B == 1 and D == D_MODEL
    L = ada_w.shape[0]
    xs = x.reshape(S, D)
    mod = _modulation(c, ada_w, ada_b)
    for l in range(L):
        sh1, sc1, g1, sh2, sc2, g2 = [mod[l, :, m * D:(m + 1) * D] for m in range(N_MOD)]
        z, zb = _inproj(xs, norm1_g[l].reshape(1, D), sc1, sh1, _layout_w_in(w_in, l))

        fcol, frow = _forget_cumsum(z, jnp.pad(fox_fbias[l], (0, LANE - FOX_HEADS)).reshape(1, LANE))
        fox = _fox_attention(z, zb, fcol, frow)

        wa2_pad = jnp.pad(gla_wa2[l], ((0, LANE - GLA_GATE_RANK), (0, 0)))
        gla = _gla(z, wa2_pad, gla_ba[l].reshape(1, GLA_W), gla_norm_g[l].reshape(1, HEAD_DIM))

        dsa = _dsa_attention(z, zb, _dsa_mask(z, zb))

        xs, h2 = _outproj(fox, gla, dsa, w_out[l].astype(BF16), xs, g1,
                          norm2_g[l].reshape(1, D), sc2, sh2)

        n, e1, r2, e2 = _peer_scores(h2, peer_wq[l].astype(BF16), peer_k1[l].astype(BF16),
                                     peer_k2[l].astype(BF16))
        vt = _transpose_cast(peer_v, l, PEER_EXPERTS, D, 0, 512, 512, "peer_v_transpose")
        xs = _peer_mix(h2, peer_u[l].astype(BF16), vt, n, e1, r2, e2, xs, g2)
    return _final_norm(xs, final_g.reshape(1, D)).reshape(B, S, D)
```

```python
import functools

import jax
import jax.numpy as jnp
from jax import lax
from jax.experimental import pallas as pl
from jax.experimental.pallas import tpu as pltpu

F32 = jnp.float32
BF16 = jnp.bfloat16
I32 = jnp.int32

D_MODEL = 2048
HEAD_DIM = 128
FOX_HEADS = 6
GLA_HEADS = 4
DSA_HEADS = 6
FOX_W = FOX_HEADS * HEAD_DIM
GLA_W = GLA_HEADS * HEAD_DIM
DSA_W = DSA_HEADS * HEAD_DIM
GLA_GATE_RANK = 16
GLA_GATE_TAU = 16.0
GLA_CHUNK = 64
IDX_HEADS = 16
IDX_DIM = 64
DSA_MAX_TOPK = 256
PEER_HEADS = 8
PEER_DQ = 256
PEER_NKEYS = 128
PEER_EXPERTS = PEER_NKEYS * PEER_NKEYS
PEER_TOPK = 16
RMS_EPS = 1e-6
N_MOD = 6
N_IN = 3 * FOX_W + FOX_HEADS + 4 * GLA_W + GLA_GATE_RANK + 3 * DSA_W + IDX_HEADS * IDX_DIM + IDX_DIM + IDX_HEADS

LANE = 128
SUBLANE = 8
NEG = -1e30
INT_MIN = -(2 ** 31)
VMEM_LIMIT = 56 * 1024 * 1024
ATTN_SCALE = HEAD_DIM ** -0.5

COL_IQ = 0
COL_GQ = 2048
COL_GK = 2560
COL_GV = 3072
COL_GR = 3584
COL_FF = 4096
COL_GA = 4224
COL_IK = 4352
COL_IW = 4480
COL_FQ = 4608
COL_FK = 5376
COL_FV = 6144
COL_DQ = 6912
COL_DK = 7680
COL_DV = 8448
N_IN_PAD = 9216


def _w_in_groups():
    sizes = (("fq", FOX_W), ("fk", FOX_W), ("fv", FOX_W), ("ff", FOX_HEADS), ("gq", GLA_W),
             ("gk", GLA_W), ("gv", GLA_W), ("gr", GLA_W), ("ga", GLA_GATE_RANK), ("dq", DSA_W),
             ("dk", DSA_W), ("dv", DSA_W), ("iq", IDX_HEADS * IDX_DIM), ("ik", IDX_DIM),
             ("iw", IDX_HEADS))
    dst = dict(fq=COL_FQ, fk=COL_FK, fv=COL_FV, ff=COL_FF, gq=COL_GQ, gk=COL_GK, gv=COL_GV,
               gr=COL_GR, ga=COL_GA, dq=COL_DQ, dk=COL_DK, dv=COL_DV, ik=COL_IK, iw=COL_IW)
    scale = dict(fq=ATTN_SCALE, dq=ATTN_SCALE, gq=ATTN_SCALE)
    groups, o = [], 0
    for name, n in sizes:
        if name == "iq":
            for h in range(IDX_HEADS):
                groups.append((o + h * IDX_DIM, IDX_DIM, COL_IQ + h * LANE, 1.0))
        else:
            groups.append((o, n, dst[name], scale.get(name, 1.0)))
        o += n
    assert o == N_IN
    return tuple(groups)


def _cparams(sem):
    return pltpu.CompilerParams(dimension_semantics=sem, vmem_limit_bytes=VMEM_LIMIT)


def _split3(x):
    hi = x.astype(BF16)
    r1 = x - hi.astype(F32)
    mid = r1.astype(BF16)
    lo = (r1 - mid.astype(F32)).astype(BF16)
    return hi, mid, lo


def _dot(a, b):
    return jnp.dot(a, b, preferred_element_type=F32)


def _dot_nt(a, b):
    return lax.dot_general(a, b, (((1,), (1,)), ((), ())), preferred_element_type=F32)


def _dot_tn(a, b):
    return lax.dot_general(a, b, (((0,), (0,)), ((), ())), preferred_element_type=F32)


def _mod_kernel(c_ref, w_ref, b_ref, o_ref):
    c = c_ref[...]
    ca = jax.nn.silu(c)
    o_ref[0] = jnp.sum(w_ref[0] * ca, axis=0, keepdims=True) + b_ref[0]


def _modulation(c, ada_w, ada_b):
    L, D, E = ada_w.shape
    tn = 1024
    return pl.pallas_call(
        _mod_kernel,
        grid=(L, E // tn),
        in_specs=[pl.BlockSpec((D, 1), lambda l, j: (0, 0)),
                  pl.BlockSpec((1, D, tn), lambda l, j: (l, 0, j)),
                  pl.BlockSpec((1, 1, tn), lambda l, j: (l, 0, j))],
        out_specs=pl.BlockSpec((1, 1, tn), lambda l, j: (l, 0, j)),
        out_shape=jax.ShapeDtypeStruct((L, 1, E), F32),
        compiler_params=_cparams(("arbitrary", "arbitrary")),
        name="adaln_mod",
    )(c.reshape(D, 1), ada_w, ada_b.reshape(L, 1, E))


def _relayout_kernel(w_ref, o_ref):
    o_ref[...] = jnp.zeros_like(o_ref)
    for src, n, dst, scale in _w_in_groups():
        w = w_ref[:, src:src + n]
        if scale != 1.0:
            w = w * scale
        o_ref[:, dst:dst + n] = w.astype(BF16)


def _layout_w_in(w_in, l):
    _, D, N = w_in.shape
    tr = 256
    return pl.pallas_call(
        _relayout_kernel,
        grid=(D // tr,),
        in_specs=[pl.BlockSpec((None, tr, N), lambda i: (l, i, 0))],
        out_specs=pl.BlockSpec((tr, N_IN_PAD), lambda i: (i, 0)),
        out_shape=jax.ShapeDtypeStruct((D, N_IN_PAD), BF16),
        compiler_params=_cparams(("arbitrary",)),
        name="w_in_relayout",
    )(w_in)


def _transpose_cast_kernel(x_ref, o_ref):
    o_ref[...] = x_ref[...].T.astype(BF16)


def _transpose_cast(x, layer, rows, cols, col0, tr, tc, name):
    if layer is None:
        spec = pl.BlockSpec((tr, tc), lambda i, j: (i, col0 // tc + j))
    else:
        spec = pl.BlockSpec((None, tr, tc), lambda i, j: (layer, i, col0 // tc + j))
    return pl.pallas_call(
        _transpose_cast_kernel,
        grid=(rows // tr, cols // tc),
        in_specs=[spec],
        out_specs=pl.BlockSpec((tc, tr), lambda i, j: (j, i)),
        out_shape=jax.ShapeDtypeStruct((cols, rows), BF16),
        compiler_params=_cparams(("arbitrary", "arbitrary")),
        name=name,
    )(x)


def _ada_norm(x, g, sc, sh):
    y = x * lax.rsqrt(jnp.mean(x * x, axis=-1, keepdims=True) + RMS_EPS) * g
    return y * (1.0 + sc) + sh


def _inproj_kernel(x_ref, g_ref, sc_ref, sh_ref, w_ref, o_ref, ob_ref, h_scr):
    @pl.when(pl.program_id(1) == 0)
    def _():
        h_scr[...] = _ada_norm(x_ref[...], g_ref[...], sc_ref[...], sh_ref[...]).astype(BF16)

    z = _dot(h_scr[...], w_ref[...])
    o_ref[...] = z
    ob_ref[...] = z.astype(BF16)


def _inproj(x, g, sc, sh, w_p):
    S, D = x.shape
    N = w_p.shape[1]
    tm, tn = 1024, 512
    vec = pl.BlockSpec((1, D), lambda i, j: (0, 0))
    out = pl.BlockSpec((tm, tn), lambda i, j: (i, j))
    return pl.pallas_call(
        _inproj_kernel,
        grid=(S // tm, N // tn),
        in_specs=[pl.BlockSpec((tm, D), lambda i, j: (i, 0)), vec, vec, vec,
                  pl.BlockSpec((D, tn), lambda i, j: (0, j))],
        out_specs=[out, out],
        out_shape=[jax.ShapeDtypeStruct((S, N), F32), jax.ShapeDtypeStruct((S, N), BF16)],
        scratch_shapes=[pltpu.VMEM((tm, D), BF16)],
        compiler_params=_cparams(("arbitrary", "arbitrary")),
        name="norm_inproj",
    )(x, g, sc, sh, w_p)


def _forget_kernel(ff_ref, fb_ref, fcol_ref, frow_ref, carry):
    @pl.when(pl.program_id(0) == 0)
    def _():
        carry[...] = jnp.zeros_like(carry)

    tm = ff_ref.shape[0]
    logf = jax.nn.log_sigmoid(ff_ref[...] + fb_ref[...])
    r = lax.broadcasted_iota(I32, (tm, tm), 0)
    c = lax.broadcasted_iota(I32, (tm, tm), 1)
    tril = jnp.where(c <= r, 1.0, 0.0).astype(BF16)
    hi, mid, lo = _split3(logf)
    cum = _dot(tril, hi) + _dot(tril, mid) + _dot(tril, lo) + carry[...]
    carry[...] = cum[tm - 1:tm, :]
    fcol_ref[...] = cum
    frow_ref[...] = cum.T[0:SUBLANE, :]


def _forget_cumsum(z, fbias_pad):
    S = z.shape[0]
    tm = 256
    return pl.pallas_call(
        _forget_kernel,
        grid=(S // tm,),
        in_specs=[pl.BlockSpec((tm, LANE), lambda i: (i, COL_FF // LANE)),
                  pl.BlockSpec((1, LANE), lambda i: (0, 0))],
        out_specs=[pl.BlockSpec((tm, LANE), lambda i: (i, 0)),
                   pl.BlockSpec((SUBLANE, tm), lambda i: (0, i))],
        out_shape=[jax.ShapeDtypeStruct((S, LANE), F32), jax.ShapeDtypeStruct((SUBLANE, S), F32)],
        scratch_shapes=[pltpu.VMEM((1, LANE), F32)],
        compiler_params=_cparams(("arbitrary",)),
        name="fox_forget_cumsum",
    )(z, fbias_pad)


def _flash_kernel(*refs, fox, nh, tq, tk):
    if fox:
        q_ref, k_ref, vt_ref, fq_ref, fk_ref, o_ref, m_scr, l_scr, acc_scr, s_scr, p_scr = refs
    else:
        q_ref, k_ref, vt_ref, b_ref, o_ref, m_scr, l_scr, acc_scr, s_scr, p_scr = refs
    i = pl.program_id(0)
    j = pl.program_id(1)

    @pl.when(j == 0)
    def _():
        m_scr[...] = jnp.full_like(m_scr, NEG)
        l_scr[...] = jnp.zeros_like(l_scr)
        acc_scr[...] = jnp.zeros_like(acc_scr)

    def step(diagonal):
        for h in range(nh):
            sl = slice(h * HEAD_DIM, (h + 1) * HEAD_DIM)
            s_scr[h] = _dot_nt(k_ref[:, sl], q_ref[:, sl])
        if fox:
            if diagonal:
                keys = lax.broadcasted_iota(I32, (tk, tq), 0)
                qrys = lax.broadcasted_iota(I32, (tk, tq), 1)
                causal = keys <= qrys
        else:
            bias = b_ref[...].astype(F32)
        m_all = m_scr[...]
        l_all = l_scr[...]
        m_rows, l_rows, alphas = [], [], []
        for h in range(nh):
            s = s_scr[h]
            if fox:
                s = s + (fq_ref[h:h + 1, :] - fk_ref[:, h:h + 1])
                if diagonal:
                    s = jnp.where(causal, s, NEG)
            else:
                s = s + bias
            m_prev = m_all[h:h + 1, :]
            m_new = jnp.maximum(m_prev, jnp.max(s, axis=0, keepdims=True))
            alpha = jnp.exp(m_prev - m_new)
            p = jnp.exp(s - m_new)
            l_rows.append(alpha * l_all[h:h + 1, :] + jnp.sum(p, axis=0, keepdims=True))
            m_rows.append(m_new)
            alphas.append(alpha)
            p_scr[h] = p.astype(BF16)
        m_scr[...] = jnp.concatenate(m_rows + [m_all[nh:, :]], axis=0)
        l_scr[...] = jnp.concatenate(l_rows + [l_all[nh:, :]], axis=0)
        for h in range(nh):
            sl = slice(h * HEAD_DIM, (h + 1) * HEAD_DIM)
            acc_scr[sl, :] = alphas[h] * acc_scr[sl, :] + _dot(vt_ref[sl, :], p_scr[h])

    if fox:
        pl.when(j < i)(lambda: step(False))
        pl.when(j == i)(lambda: step(True))
    else:
        pl.when(j <= i)(lambda: step(False))

    @pl.when(j == i)
    def _():
        for h in range(nh):
            sl = slice(h * HEAD_DIM, (h + 1) * HEAD_DIM)
            o_ref[:, sl] = (acc_scr[sl, :] / l_scr[h:h + 1, :]).T


def _flash(zb, vt, col_q, col_k, nh, *, fcol=None, frow=None, bias_t=None):
    S = zb.shape[0]
    W = nh * HEAD_DIM
    tq = tk = 256
    fox = bias_t is None
    kj = lambda i, j: jnp.minimum(j, i)
    qspec = pl.BlockSpec((tq, W), lambda i, j: (i, col_q // W))
    kspec = pl.BlockSpec((tk, W), lambda i, j: (kj(i, j), col_k // W))
    vspec = pl.BlockSpec((W, tk), lambda i, j: (0, kj(i, j)))
    if fox:
        extra = [frow, fcol]
        especs = [pl.BlockSpec((SUBLANE, tq), lambda i, j: (0, i)),
                  pl.BlockSpec((tk, LANE), lambda i, j: (kj(i, j), 0))]
    else:
        extra = [bias_t]
        especs = [pl.BlockSpec((tk, tq), lambda i, j: (kj(i, j), i))]
    return pl.pallas_call(
        functools.partial(_flash_kernel, fox=fox, nh=nh, tq=tq, tk=tk),
        grid=(S // tq, S // tk),
        in_specs=[qspec, kspec, vspec] + especs,
        out_specs=pl.BlockSpec((tq, W), lambda i, j: (i, 0)),
        out_shape=jax.ShapeDtypeStruct((S, W), F32),
        scratch_shapes=[pltpu.VMEM((SUBLANE, tq), F32), pltpu.VMEM((SUBLANE, tq), F32),
                        pltpu.VMEM((W, tq), F32), pltpu.VMEM((nh, tk, tq), F32),
                        pltpu.VMEM((nh, tk, tq), BF16)],
        compiler_params=_cparams(("arbitrary", "arbitrary")),
        name="fox_flash" if fox else "dsa_flash",
    )(zb, zb, vt, *extra)


def _fox_attention(z, zb, fcol, frow):
    vt = _transpose_cast(z, None, z.shape[0], FOX_W, COL_FV, 256, 256, "fox_v_transpose")
    return _flash(zb, vt, COL_FQ, COL_FK, FOX_HEADS, fcol=fcol, frow=frow)


def _dsa_attention(z, zb, bias_t):
    vt = _transpose_cast(z, None, z.shape[0], DSA_W, COL_DV, 256, 256, "dsa_v_transpose")
    return _flash(zb, vt, COL_DQ, COL_DK, DSA_HEADS, bias_t=bias_t)


def _gla_kernel(q_ref, k_ref, v_ref, r_ref, a_ref, wa_ref, ba_ref, gn_ref, o_ref, st_scr):
    @pl.when(pl.program_id(1) == 0)
    def _():
        st_scr[...] = jnp.zeros_like(st_scr)

    C = GLA_CHUNK
    tm = q_ref.shape[0]
    r = lax.broadcasted_iota(I32, (C, C), 0)
    c = lax.broadcasted_iota(I32, (C, C), 1)
    lower = c <= r
    tril = jnp.where(lower, 1.0, 0.0).astype(BF16)

    a_hi, a_mid, _ = _split3(a_ref[...])
    w_hi, w_mid, _ = _split3(wa_ref[...])
    gate_x = _dot(a_hi, w_hi) + (_dot(a_hi, w_mid) + _dot(a_mid, w_hi)) + ba_ref[...]
    log_a = jax.nn.log_sigmoid(gate_x) / GLA_GATE_TAU

    st = st_scr[...]
    for ci in range(tm // C):
        rows = slice(ci * C, (ci + 1) * C)
        la_hi, la_mid, la_lo = _split3(log_a[rows])
        g = _dot(tril, la_hi) + _dot(tril, la_mid) + _dot(tril, la_lo)
        q = q_ref[rows, :]
        k = k_ref[rows, :]
        v = v_ref[rows, :].astype(BF16)
        g_mid = g[C // 2:C // 2 + 1, :]
        g_last = g[C - 1:C, :]
        inter = _dot_nt((q * jnp.exp(g)).astype(BF16), st.astype(BF16))
        qr = (q * jnp.exp(jnp.minimum(g - g_mid, 80.0))).astype(BF16)
        kr = (k * jnp.exp(jnp.minimum(g_mid - g, 80.0))).astype(BF16)
        att = jnp.where(lower, _dot_nt(qr, kr), 0.0)
        intra = _dot(att.astype(BF16), v)
        kd = (k * jnp.exp(g_last - g)).astype(BF16)
        st = st * jnp.exp(g_last) + _dot_tn(v, kd)
        go = inter + intra
        y = go * lax.rsqrt(jnp.mean(go * go, axis=-1, keepdims=True) + RMS_EPS) * gn_ref[...]
        o_ref[rows, :] = y * jax.nn.silu(r_ref[rows, :])
    st_scr[...] = st


def _gla(z, wa2_pad, ba, gnorm):
    S = z.shape[0]
    tm = 256
    hb = lambda col: (lambda h, i: (i, col // LANE + h))
    return pl.pallas_call(
        _gla_kernel,
        grid=(GLA_HEADS, S // tm),
        in_specs=[pl.BlockSpec((tm, LANE), hb(COL_GQ)), pl.BlockSpec((tm, LANE), hb(COL_GK)),
                  pl.BlockSpec((tm, LANE), hb(COL_GV)), pl.BlockSpec((tm, LANE), hb(COL_GR)),
                  pl.BlockSpec((tm, LANE), lambda h, i: (i, COL_GA // LANE)),
                  pl.BlockSpec((LANE, LANE), lambda h, i: (0, h)),
                  pl.BlockSpec((1, LANE), lambda h, i: (0, h)),
                  pl.BlockSpec((1, LANE), lambda h, i: (0, 0))],
        out_specs=pl.BlockSpec((tm, LANE), lambda h, i: (i, h)),
        out_shape=jax.ShapeDtypeStruct((S, GLA_W), F32),
        scratch_shapes=[pltpu.VMEM((HEAD_DIM, HEAD_DIM), F32)],
        compiler_params=_cparams(("arbitrary", "arbitrary")),
        name="gla_chunked",
    )(z, z, z, z, z, wa2_pad, ba, gnorm)


def _sort_key(x):
    u = pltpu.bitcast(x, I32)
    return u ^ ((u >> 31) & 0x7FFFFFFF)


def _indexer_kernel(iq_ref, ik_ref, iw_ref, o_ref, key_scr, *, tq, topk):
    i = pl.program_id(0)
    S = ik_ref.shape[0]
    nk = S // tq
    wt = iw_ref[...].T[0:IDX_HEADS, :] * (IDX_HEADS ** -0.5 * IDX_DIM ** -0.5)
    qrys = i * tq + lax.broadcasted_iota(I32, (tq, tq), 1)
    key0 = lax.broadcasted_iota(I32, (tq, tq), 0)

    def score_chunk(jc, carry):
        off = pl.multiple_of(jc * tq, tq)
        ikb = ik_ref[pl.ds(off, tq), :]
        acc = jnp.zeros((tq, tq), F32)
        for h in range(IDX_HEADS):
            qh = iq_ref[:, h * LANE:(h + 1) * LANE]
            acc = acc + jnp.maximum(_dot_nt(ikb, qh), 0.0) * wt[h:h + 1, :]
        key_scr[pl.ds(off, tq), :] = jnp.where(off + key0 <= qrys, _sort_key(acc), INT_MIN)
        return carry

    lax.fori_loop(0, i + 1, score_chunk, 0)

    def bisect(it, thr):
        cand = thr + lax.shift_left(jnp.int32(1), 31 - it)

        def count_chunk(jc, cnt):
            off = pl.multiple_of(jc * tq, tq)
            ge = jnp.where(key_scr[pl.ds(off, tq), :] >= cand, 1, 0)
            return cnt + jnp.sum(ge.reshape(tq // SUBLANE, SUBLANE, tq), axis=0)

        cnt = lax.fori_loop(0, i + 1, count_chunk, jnp.zeros((SUBLANE, tq), I32))
        return jnp.where(jnp.sum(cnt, axis=0, keepdims=True) >= topk, cand, thr)

    thr = lax.fori_loop(0, 32, bisect, jnp.full((1, tq), INT_MIN, I32))
    thr = jnp.maximum(thr, INT_MIN + 1)

    def write_chunk(jc, carry):
        off = pl.multiple_of(jc * tq, tq)
        keep = key_scr[pl.ds(off, tq), :] >= thr
        o_ref[pl.ds(off, tq), :] = jnp.where(keep, 0.0, NEG).astype(BF16)
        return carry

    lax.fori_loop(0, i + 1, write_chunk, 0)

    def fill_chunk(jc, carry):
        off = pl.multiple_of(jc * tq, tq)
        o_ref[pl.ds(off, tq), :] = jnp.full((tq, tq), NEG, BF16)
        return carry

    lax.fori_loop(i + 1, nk, fill_chunk, 0)


def _dsa_mask(z, zb):
    S = z.shape[0]
    tq = 256
    topk = min(DSA_MAX_TOPK, S // 4)
    return pl.pallas_call(
        functools.partial(_indexer_kernel, tq=tq, topk=topk),
        grid=(S // tq,),
        in_specs=[pl.BlockSpec((tq, IDX_HEADS * LANE), lambda i: (i, COL_IQ // (IDX_HEADS * LANE))),
                  pl.BlockSpec((S, LANE), lambda i: (0, COL_IK // LANE)),
                  pl.BlockSpec((tq, LANE), lambda i: (i, COL_IW // LANE))],
        out_specs=pl.BlockSpec((S, tq), lambda i: (0, i)),
        out_shape=jax.ShapeDtypeStruct((S, S), BF16),
        scratch_shapes=[pltpu.VMEM((S, tq), I32)],
        compiler_params=_cparams(("arbitrary",)),
        name="dsa_indexer_mask",
    )(zb, zb, z)


def _outproj_kernel(fox_ref, gla_ref, dsa_ref, w_ref, x_ref, g1_ref, n2_ref, sc_ref, sh_ref,
                    xo_ref, h_ref):
    acc = _dot(fox_ref[...].astype(BF16), w_ref[0:FOX_W, :])
    acc = acc + _dot(gla_ref[...].astype(BF16), w_ref[FOX_W:FOX_W + GLA_W, :])
    acc = acc + _dot(dsa_ref[...].astype(BF16), w_ref[FOX_W + GLA_W:, :])
    xn = x_ref[...] + g1_ref[...] * acc
    xo_ref[...] = xn
    h_ref[...] = _ada_norm(xn, n2_ref[...], sc_ref[...], sh_ref[...]).astype(BF16)


def _outproj(fox, gla, dsa, w_out, x, g1, n2, sc2, sh2):
    S, D = x.shape
    tm = 256
    vec = pl.BlockSpec((1, D), lambda i: (0, 0))
    row = lambda w: pl.BlockSpec((tm, w), lambda i: (i, 0))
    return pl.pallas_call(
        _outproj_kernel,
        grid=(S // tm,),
        in_specs=[row(FOX_W), row(GLA_W), row(DSA_W), pl.BlockSpec((D, D), lambda i: (0, 0)),
                  row(D), vec, vec, vec, vec],
        out_specs=[row(D), row(D)],
        out_shape=[jax.ShapeDtypeStruct((S, D), F32), jax.ShapeDtypeStruct((S, D), BF16)],
        compiler_params=_cparams(("arbitrary",)),
        name="outproj_residual_norm",
    )(fox, gla, dsa, w_out, x, g1, n2, sc2, sh2)


def _top_rows(x, n):
    rows = x.shape[0]
    ridx = lax.broadcasted_iota(I32, x.shape, 0)
    rank = jnp.full(x.shape, float(n), F32)
    vals = []
    for k in range(n):
        m = jnp.max(x, axis=0, keepdims=True)
        first = jnp.min(jnp.where(x == m, ridx, rows), axis=0, keepdims=True)
        sel = ridx == first
        x = jnp.where(sel, -jnp.inf, x)
        rank = jnp.where(sel, float(k), rank)
        vals.append(m)
    return jnp.concatenate(vals, axis=0), rank


def _peer_score_kernel(h_ref, wq_ref, k1_ref, k2_ref, n_ref, e1_ref, r2_ref, e2_ref, q_scr):
    q = _dot(h_ref[...], wq_ref[...])
    for h in range(PEER_HEADS):
        q_scr[h] = q[:, h * PEER_DQ:(h + 1) * PEER_DQ].astype(BF16)
    half = PEER_DQ // 2
    K = PEER_TOPK

    for h in range(PEER_HEADS):
        qh = q_scr[h]
        s1 = _dot_nt(k1_ref[h], qh[:, :half])
        s2 = _dot_nt(k2_ref[h], qh[:, half:])
        v1, r1 = _top_rows(s1, K)
        v2, r2 = _top_rows(s2, K)
        groups = ([v1[0:1, :] + v2] + [v1[a:a + 1, :] + v2[0:SUBLANE, :] for a in range(1, SUBLANE)]
                  + [v1[SUBLANE:K, :] + v2[0:1, :]])
        best, _ = _top_rows(jnp.concatenate(groups, axis=0), K)
        tau = best[K - 1:K, :]
        zsum = jnp.sum(jnp.exp(best - best[0:1, :]), axis=0, keepdims=True)
        n = jnp.zeros(s1.shape, F32)
        for a in range(SUBLANE):
            n_a = jnp.sum(jnp.where(groups[a] >= tau, 1.0, 0.0), axis=0, keepdims=True)
            n = jnp.where(r1 == float(a), n_a, n)
        tail = jnp.where(groups[SUBLANE] >= tau, 1.0, 0.0)
        for a in range(SUBLANE, K):
            n = jnp.where(r1 == float(a), tail[a - SUBLANE:a - SUBLANE + 1, :], n)
        n_ref[h] = n
        e1_ref[h] = jnp.where(r1 < float(K), jnp.exp(s1 - v1[0:1, :]), 0.0)
        r2_ref[h] = r2.astype(BF16)
        e2_ref[h] = (jnp.where(r2 < float(K), jnp.exp(s2 - v2[0:1, :]), 0.0) / zsum).astype(BF16)


def _peer_scores(h2, wq, k1, k2):
    S, D = h2.shape
    tm = 256
    big = pl.BlockSpec((PEER_HEADS, PEER_NKEYS, tm), lambda i: (0, 0, i))
    kspec = pl.BlockSpec((PEER_HEADS, PEER_NKEYS, PEER_DQ // 2), lambda i: (0, 0, 0))
    f32s = jax.ShapeDtypeStruct((PEER_HEADS, PEER_NKEYS, S), F32)
    b16s = jax.ShapeDtypeStruct((PEER_HEADS, PEER_NKEYS, S), BF16)
    return pl.pallas_call(
        _peer_score_kernel,
        grid=(S // tm,),
        in_specs=[pl.BlockSpec((tm, D), lambda i: (i, 0)),
                  pl.BlockSpec((D, PEER_HEADS * PEER_DQ), lambda i: (0, 0)), kspec, kspec],
        out_specs=[big, big, big, big],
        out_shape=[f32s, f32s, b16s, b16s],
        scratch_shapes=[pltpu.VMEM((PEER_HEADS, tm, PEER_DQ), BF16)],
        compiler_params=_cparams(("arbitrary",)),
        name="peer_scores",
    )(h2, wq, k1, k2)


def _peer_mix_kernel(h_ref, u_ref, vta_ref, vtb_ref, n_ref, e1_ref, r2_ref, e2_ref, x_ref, g2_ref,
                     o_ref, acc_scr, p_scr, w_scr, *, eb, nb):
    s = pl.program_id(1)
    last = pl.num_programs(1) - 1
    ng = eb // PEER_NKEYS
    zero = jnp.zeros((), BF16)

    @pl.when(s == 0)
    def _():
        acc_scr[...] = jnp.zeros_like(acc_scr)
        p_scr[1] = jnp.zeros(p_scr.shape[1:], F32)

    def produce(slot):
        p_scr[slot] = _dot_nt(u_ref[slot * eb:(slot + 1) * eb, :], h_ref[...])

    def consume(slot, blk, vt_ref):
        for cc in range(ng):
            rows = slice(cc * PEER_NKEYS, (cc + 1) * PEER_NKEYS)
            c = blk * ng + cc
            g = None
            for h in range(PEER_HEADS):
                keep = r2_ref[h] < n_ref[h, pl.ds(c, 1), :].astype(BF16)
                gh = jnp.where(keep, e2_ref[h], zero) * e1_ref[h, pl.ds(c, 1), :].astype(BF16)
                g = gh if g is None else g + gh
            w_scr[slot, rows, :] = g * jax.nn.gelu(p_scr[slot, rows, :]).astype(BF16)
        acc_scr[...] += _dot(vt_ref[...], w_scr[slot])

    produce(0)
    consume(1, jnp.maximum(2 * s - 1, 0), vta_ref)

    @pl.when(s < last)
    def _():
        produce(1)
        consume(0, jnp.minimum(2 * s, nb - 1), vtb_ref)

    @pl.when(s == last)
    def _():
        o_ref[...] = x_ref[...] + g2_ref[...] * acc_scr[...].T


def _peer_mix(h2, u, vt, n, e1, r2, e2, x, g2):
    S, D = x.shape
    tm, eb = 512, 512
    nb = PEER_EXPERTS // eb
    once = dict(pipeline_mode=pl.Buffered(1))
    big = pl.BlockSpec((PEER_HEADS, PEER_NKEYS, tm), lambda i, s: (0, 0, i), **once)
    row = pl.BlockSpec((tm, D), lambda i, s: (i, 0), **once)
    return pl.pallas_call(
        functools.partial(_peer_mix_kernel, eb=eb, nb=nb),
        grid=(S // tm, nb // 2 + 1),
        in_specs=[row,
                  pl.BlockSpec((2 * eb, D), lambda i, s: (jnp.minimum(s, nb // 2 - 1), 0)),
                  pl.BlockSpec((D, eb), lambda i, s: (0, jnp.maximum(2 * s - 1, 0))),
                  pl.BlockSpec((D, eb), lambda i, s: (0, jnp.minimum(2 * s, nb - 1))),
                  big, big, big, big, row, pl.BlockSpec((1, D), lambda i, s: (0, 0))],
        out_specs=pl.BlockSpec((tm, D), lambda i, s: (i, 0)),
        out_shape=jax.ShapeDtypeStruct((S, D), F32),
        scratch_shapes=[pltpu.VMEM((D, tm), F32), pltpu.VMEM((2, eb, tm), F32),
                        pltpu.VMEM((2, eb, tm), BF16)],
        compiler_params=_cparams(("arbitrary", "arbitrary")),
        name="peer_dense_mix",
    )(h2, u, vt, vt, n, e1, r2, e2, x, g2)


def _final_norm_kernel(x_ref, g_ref, o_ref):
    x = x_ref[...]
    o_ref[...] = x * lax.rsqrt(jnp.mean(x * x, axis=-1, keepdims=True) + RMS_EPS) * g_ref[...]


def _final_norm(x, g):
    S, D = x.shape
    tm = 512
    return pl.pallas_call(
        _final_norm_kernel,
        grid=(S // tm,),
        in_specs=[pl.BlockSpec((tm, D), lambda i: (i, 0)), pl.BlockSpec((1, D), lambda i: (0, 0))],
        out_specs=pl.BlockSpec((tm, D), lambda i: (i, 0)),
        out_shape=jax.ShapeDtypeStruct((S, D), F32),
        compiler_params=_cparams(("arbitrary",)),
        name="final_rmsnorm",
    )(x, g)


def kernel(x, c, ada_w, ada_b, norm1_g, norm2_g, final_g, w_in, fox_fbias, gla_wa2, gla_ba,
           gla_norm_g, w_out, peer_wq, peer_k1, peer_k2, peer_u, peer_v):
    B, S, D = x.shape
    assert B == 1 and D == D_MODEL
    L = ada_w.shape[0]
    xs = x.reshape(S, D)
    mod = _modulation(c, ada_w, ada_b)
    for l in range(L):
        sh1, sc1, g1, sh2, sc2, g2 = [mod[l, :, m * D:(m + 1) * D] for m in range(N_MOD)]
        z, zb = _inproj(xs, norm1_g[l].reshape(1, D), sc1, sh1, _layout_w_in(w_in, l))

        fcol, frow = _forget_cumsum(z, jnp.pad(fox_fbias[l], (0, LANE - FOX_HEADS)).reshape(1, LANE))
        fox = _fox_attention(z, zb, fcol, frow)

        wa2_pad = jnp.pad(gla_wa2[l], ((0, LANE - GLA_GATE_RANK), (0, 0)))
        gla = _gla(z, wa2_pad, gla_ba[l].reshape(1, GLA_W), gla_norm_g[l].reshape(1, HEAD_DIM))

        dsa = _dsa_attention(z, zb, _dsa_mask(z, zb))

        xs, h2 = _outproj(fox, gla, dsa, w_out[l].astype(BF16), xs, g1,
                          norm2_g[l].reshape(1, D), sc2, sh2)

        n, e1, r2, e2 = _peer_scores(h2, peer_wq[l].astype(BF16), peer_k1[l].astype(BF16),
                                     peer_k2[l].astype(BF16))
        vt = _transpose_cast(peer_v, l, PEER_EXPERTS, D, 0, 512, 512, "peer_v_transpose")
        xs = _peer_mix(h2, peer_u[l].astype(BF16), vt, n, e1, r2, e2, xs, g2)
    return _final_norm(xs, final_g.reshape(1, D)).reshape(B, S, D)
```

```python
import functools

import jax
import jax.numpy as jnp
from jax import lax
from jax.experimental import pallas as pl
from jax.experimental.pallas import tpu as pltpu

F32 = jnp.float32
BF16 = jnp.bfloat16
I32 = jnp.int32

D_MODEL = 2048
HEAD_DIM = 128
FOX_HEADS = 6
GLA_HEADS = 4
DSA_HEADS = 6
FOX_W = FOX_HEADS * HEAD_DIM
GLA_W = GLA_HEADS * HEAD_DIM
DSA_W = DSA_HEADS * HEAD_DIM
GLA_GATE_RANK = 16
GLA_GATE_TAU = 16.0
GLA_CHUNK = 64
IDX_HEADS = 16
IDX_DIM = 64
DSA_MAX_TOPK = 256
PEER_HEADS = 8
PEER_DQ = 256
PEER_NKEYS = 128
PEER_EXPERTS = PEER_NKEYS * PEER_NKEYS
PEER_TOPK = 16
RMS_EPS = 1e-6
N_MOD = 6
N_IN = 3 * FOX_W + FOX_HEADS + 4 * GLA_W + GLA_GATE_RANK + 3 * DSA_W + IDX_HEADS * IDX_DIM + IDX_DIM + IDX_HEADS

LANE = 128
SUBLANE = 8
NEG = -1e30
INT_MIN = -(2 ** 31)
VMEM_LIMIT = 56 * 1024 * 1024
ATTN_SCALE = HEAD_DIM ** -0.5

COL_IQ = 0
COL_GQ = 2048
COL_GK = 2560
COL_GV = 3072
COL_GR = 3584
COL_FF = 4096
COL_GA = 4224
COL_IK = 4352
COL_IW = 4480
COL_FQ = 4608
COL_FK = 5376
COL_FV = 6144
COL_DQ = 6912
COL_DK = 7680
COL_DV = 8448
N_IN_PAD = 9216


def _w_in_groups():
    sizes = (("fq", FOX_W), ("fk", FOX_W), ("fv", FOX_W), ("ff", FOX_HEADS), ("gq", GLA_W),
             ("gk", GLA_W), ("gv", GLA_W), ("gr", GLA_W), ("ga", GLA_GATE_RANK), ("dq", DSA_W),
             ("dk", DSA_W), ("dv", DSA_W), ("iq", IDX_HEADS * IDX_DIM), ("ik", IDX_DIM),
             ("iw", IDX_HEADS))
    dst = dict(fq=COL_FQ, fk=COL_FK, fv=COL_FV, ff=COL_FF, gq=COL_GQ, gk=COL_GK, gv=COL_GV,
               gr=COL_GR, ga=COL_GA, dq=COL_DQ, dk=COL_DK, dv=COL_DV, ik=COL_IK, iw=COL_IW)
    scale = dict(fq=ATTN_SCALE, dq=ATTN_SCALE, gq=ATTN_SCALE)
    groups, o = [], 0
    for name, n in sizes:
        if name == "iq":
            for h in range(IDX_HEADS):
                groups.append((o + h * IDX_DIM, IDX_DIM, COL_IQ + h * LANE, 1.0))
        else:
            groups.append((o, n, dst[name], scale.get(name, 1.0)))
        o += n
    assert o == N_IN
    return tuple(groups)


def _cparams(sem):
    return pltpu.CompilerParams(dimension_semantics=sem, vmem_limit_bytes=VMEM_LIMIT)


def _split3(x):
    hi = x.astype(BF16)
    r1 = x - hi.astype(F32)
    mid = r1.astype(BF16)
    lo = (r1 - mid.astype(F32)).astype(BF16)
    return hi, mid, lo


def _dot(a, b):
    return jnp.dot(a, b, preferred_element_type=F32)


def _dot_nt(a, b):
    return lax.dot_general(a, b, (((1,), (1,)), ((), ())), preferred_element_type=F32)


def _dot_tn(a, b):
    return lax.dot_general(a, b, (((0,), (0,)), ((), ())), preferred_element_type=F32)


def _mod_kernel(c_ref, w_ref, b_ref, o_ref):
    c = c_ref[...]
    ca = jax.nn.silu(c)
    o_ref[0] = jnp.sum(w_ref[0] * ca, axis=0, keepdims=True) + b_ref[0]


def _modulation(c, ada_w, ada_b):
    L, D, E = ada_w.shape
    tn = 1024
    return pl.pallas_call(
        _mod_kernel,
        grid=(L, E // tn),
        in_specs=[pl.BlockSpec((D, 1), lambda l, j: (0, 0)),
                  pl.BlockSpec((1, D, tn), lambda l, j: (l, 0, j)),
                  pl.BlockSpec((1, 1, tn), lambda l, j: (l, 0, j))],
        out_specs=pl.BlockSpec((1, 1, tn), lambda l, j: (l, 0, j)),
        out_shape=jax.ShapeDtypeStruct((L, 1, E), F32),
        compiler_params=_cparams(("arbitrary", "arbitrary")),
        name="adaln_mod",
    )(c.reshape(D, 1), ada_w, ada_b.reshape(L, 1, E))


def _relayout_kernel(w_ref, o_ref):
    o_ref[...] = jnp.zeros_like(o_ref)
    for src, n, dst, scale in _w_in_groups():
        w = w_ref[:, src:src + n]
        if scale != 1.0:
            w = w * scale
        o_ref[:, dst:dst + n] = w.astype(BF16)


def _layout_w_in(w_in, l):
    _, D, N = w_in.shape
    tr = 256
    return pl.pallas_call(
        _relayout_kernel,
        grid=(D // tr,),
        in_specs=[pl.BlockSpec((None, tr, N), lambda i: (l, i, 0))],
        out_specs=pl.BlockSpec((tr, N_IN_PAD), lambda i: (i, 0)),
        out_shape=jax.ShapeDtypeStruct((D, N_IN_PAD), BF16),
        compiler_params=_cparams(("arbitrary",)),
        name="w_in_relayout",
    )(w_in)


def _cast_kernel(x_ref, o_ref):
    o_ref[...] = x_ref[...].astype(BF16)


def _cast_layer(x, layer, name):
    _, R, C = x.shape
    tr = 512
    return pl.pallas_call(
        _cast_kernel,
        grid=(R // tr,),
        in_specs=[pl.BlockSpec((None, tr, C), lambda i: (layer, i, 0))],
        out_specs=pl.BlockSpec((tr, C), lambda i: (i, 0)),
        out_shape=jax.ShapeDtypeStruct((R, C), BF16),
        compiler_params=_cparams(("arbitrary",)),
        name=name,
    )(x)


def _transpose_cast_kernel(x_ref, o_ref):
    o_ref[...] = x_ref[...].T.astype(BF16)


def _transpose_cast(x, layer, rows, cols, col0, tr, tc, name):
    if layer is None:
        spec = pl.BlockSpec((tr, tc), lambda i, j: (i, col0 // tc + j))
    else:
        spec = pl.BlockSpec((None, tr, tc), lambda i, j: (layer, i, col0 // tc + j))
    return pl.pallas_call(
        _transpose_cast_kernel,
        grid=(rows // tr, cols // tc),
        in_specs=[spec],
        out_specs=pl.BlockSpec((tc, tr), lambda i, j: (j, i)),
        out_shape=jax.ShapeDtypeStruct((cols, rows), BF16),
        compiler_params=_cparams(("arbitrary", "arbitrary")),
        name=name,
    )(x)


def _ada_norm(x, g, sc, sh):
    y = x * lax.rsqrt(jnp.mean(x * x, axis=-1, keepdims=True) + RMS_EPS) * g
    return y * (1.0 + sc) + sh


def _inproj_kernel(x_ref, g_ref, sc_ref, sh_ref, w_ref, o_ref, ob_ref, h_scr):
    @pl.when(pl.program_id(1) == 0)
    def _():
        h_scr[...] = _ada_norm(x_ref[...], g_ref[...], sc_ref[...], sh_ref[...]).astype(BF16)

    z = _dot(h_scr[...], w_ref[...])
    o_ref[...] = z
    ob_ref[...] = z.astype(BF16)


def _inproj(x, g, sc, sh, w_p):
    S, D = x.shape
    N = w_p.shape[1]
    tm, tn = 1024, 512
    vec = pl.BlockSpec((1, D), lambda i, j: (0, 0))
    out = pl.BlockSpec((tm, tn), lambda i, j: (i, j))
    return pl.pallas_call(
        _inproj_kernel,
        grid=(S // tm, N // tn),
        in_specs=[pl.BlockSpec((tm, D), lambda i, j: (i, 0)), vec, vec, vec,
                  pl.BlockSpec((D, tn), lambda i, j: (0, j))],
        out_specs=[out, out],
        out_shape=[jax.ShapeDtypeStruct((S, N), F32), jax.ShapeDtypeStruct((S, N), BF16)],
        scratch_shapes=[pltpu.VMEM((tm, D), BF16)],
        compiler_params=_cparams(("arbitrary", "arbitrary")),
        name="norm_inproj",
    )(x, g, sc, sh, w_p)


def _forget_kernel(ff_ref, fb_ref, fcol_ref, frow_ref, carry):
    @pl.when(pl.program_id(0) == 0)
    def _():
        carry[...] = jnp.zeros_like(carry)

    tm = ff_ref.shape[0]
    logf = jax.nn.log_sigmoid(ff_ref[...] + fb_ref[...])
    r = lax.broadcasted_iota(I32, (tm, tm), 0)
    c = lax.broadcasted_iota(I32, (tm, tm), 1)
    tril = jnp.where(c <= r, 1.0, 0.0).astype(BF16)
    hi, mid, lo = _split3(logf)
    cum = _dot(tril, hi) + _dot(tril, mid) + _dot(tril, lo) + carry[...]
    carry[...] = cum[tm - 1:tm, :]
    fcol_ref[...] = cum
    frow_ref[...] = cum.T[0:SUBLANE, :]


def _forget_cumsum(z, fbias_pad):
    S = z.shape[0]
    tm = 256
    return pl.pallas_call(
        _forget_kernel,
        grid=(S // tm,),
        in_specs=[pl.BlockSpec((tm, LANE), lambda i: (i, COL_FF // LANE)),
                  pl.BlockSpec((1, LANE), lambda i: (0, 0))],
        out_specs=[pl.BlockSpec((tm, LANE), lambda i: (i, 0)),
                   pl.BlockSpec((SUBLANE, tm), lambda i: (0, i))],
        out_shape=[jax.ShapeDtypeStruct((S, LANE), F32), jax.ShapeDtypeStruct((SUBLANE, S), F32)],
        scratch_shapes=[pltpu.VMEM((1, LANE), F32)],
        compiler_params=_cparams(("arbitrary",)),
        name="fox_forget_cumsum",
    )(z, fbias_pad)


def _flash_kernel(*refs, fox, nh, tq, tk):
    if fox:
        q_ref, k_ref, vt_ref, fq_ref, fk_ref, o_ref, m_scr, l_scr, acc_scr, s_scr, p_scr = refs
    else:
        q_ref, k_ref, vt_ref, b_ref, o_ref, m_scr, l_scr, acc_scr, s_scr, p_scr = refs
    i = pl.program_id(0)
    j = pl.program_id(1)

    @pl.when(j == 0)
    def _():
        m_scr[...] = jnp.full_like(m_scr, NEG)
        l_scr[...] = jnp.zeros_like(l_scr)
        acc_scr[...] = jnp.zeros_like(acc_scr)

    def step(diagonal):
        for h in range(nh):
            sl = slice(h * HEAD_DIM, (h + 1) * HEAD_DIM)
            s_scr[h] = _dot_nt(k_ref[:, sl], q_ref[:, sl])
        if fox:
            if diagonal:
                keys = j * tk + lax.broadcasted_iota(I32, (tk, tq), 0)
                qrys = i * tq + lax.broadcasted_iota(I32, (tk, tq), 1)
                causal = keys <= qrys
        else:
            bias = b_ref[...].astype(F32)
        m_all = m_scr[...]
        l_all = l_scr[...]
        m_rows, l_rows, alphas = [], [], []
        for h in range(nh):
            s = s_scr[h]
            if fox:
                s = s + (fq_ref[h:h + 1, :] - fk_ref[:, h:h + 1])
                if diagonal:
                    s = jnp.where(causal, s, NEG)
            else:
                s = s + bias
            m_prev = m_all[h:h + 1, :]
            m_new = jnp.maximum(m_prev, jnp.max(s, axis=0, keepdims=True))
            alpha = jnp.exp(m_prev - m_new)
            p = jnp.exp(s - m_new)
            l_rows.append(alpha * l_all[h:h + 1, :] + jnp.sum(p, axis=0, keepdims=True))
            m_rows.append(m_new)
            alphas.append(alpha)
            p_scr[h] = p.astype(BF16)
        m_scr[...] = jnp.concatenate(m_rows + [m_all[nh:, :]], axis=0)
        l_scr[...] = jnp.concatenate(l_rows + [l_all[nh:, :]], axis=0)
        for h in range(nh):
            sl = slice(h * HEAD_DIM, (h + 1) * HEAD_DIM)
            acc_scr[sl, :] = alphas[h] * acc_scr[sl, :] + _dot(vt_ref[sl, :], p_scr[h])

    r = tq // tk
    j_last = r * i + (r - 1)
    if fox:
        pl.when(j < r * i)(lambda: step(False))
        pl.when(jnp.logical_and(j >= r * i, j <= j_last))(lambda: step(True))
    else:
        pl.when(j <= j_last)(lambda: step(False))

    @pl.when(j == j_last)
    def _():
        for h in range(nh):
            sl = slice(h * HEAD_DIM, (h + 1) * HEAD_DIM)
            o_ref[:, sl] = (acc_scr[sl, :] / l_scr[h:h + 1, :]).T


def _flash(zb, vt, col_q, col_k, nh, *, fcol=None, frow=None, bias_t=None):
    S = zb.shape[0]
    W = nh * HEAD_DIM
    tq, tk = 512, 256
    fox = bias_t is None
    kj = lambda i, j: jnp.minimum(j, (tq // tk) * i + (tq // tk - 1))
    qspec = pl.BlockSpec((tq, W), lambda i, j: (i, col_q // W))
    kspec = pl.BlockSpec((tk, W), lambda i, j: (kj(i, j), col_k // W))
    vspec = pl.BlockSpec((W, tk), lambda i, j: (0, kj(i, j)))
    if fox:
        extra = [frow, fcol]
        especs = [pl.BlockSpec((SUBLANE, tq), lambda i, j: (0, i)),
                  pl.BlockSpec((tk, LANE), lambda i, j: (kj(i, j), 0))]
    else:
        extra = [bias_t]
        especs = [pl.BlockSpec((tk, tq), lambda i, j: (kj(i, j), i))]
    return pl.pallas_call(
        functools.partial(_flash_kernel, fox=fox, nh=nh, tq=tq, tk=tk),
        grid=(S // tq, S // tk),
        in_specs=[qspec, kspec, vspec] + especs,
        out_specs=pl.BlockSpec((tq, W), lambda i, j: (i, 0)),
        out_shape=jax.ShapeDtypeStruct((S, W), F32),
        scratch_shapes=[pltpu.VMEM((SUBLANE, tq), F32), pltpu.VMEM((SUBLANE, tq), F32),
                        pltpu.VMEM((W, tq), F32), pltpu.VMEM((nh, tk, tq), F32),
                        pltpu.VMEM((nh, tk, tq), BF16)],
        compiler_params=_cparams(("arbitrary", "arbitrary")),
        name="fox_flash" if fox else "dsa_flash",
    )(zb, zb, vt, *extra)


def _fox_attention(z, zb, fcol, frow):
    vt = _transpose_cast(z, None, z.shape[0], FOX_W, COL_FV, 256, 256, "fox_v_transpose")
    return _flash(zb, vt, COL_FQ, COL_FK, FOX_HEADS, fcol=fcol, frow=frow)


def _dsa_attention(z, zb, bias_t):
    vt = _transpose_cast(z, None, z.shape[0], DSA_W, COL_DV, 256, 256, "dsa_v_transpose")
    return _flash(zb, vt, COL_DQ, COL_DK, DSA_HEADS, bias_t=bias_t)


def _gla_kernel(q_ref, k_ref, v_ref, r_ref, a_ref, wa_ref, ba_ref, gn_ref, o_ref, st_scr):
    @pl.when(pl.program_id(1) == 0)
    def _():
        st_scr[...] = jnp.zeros_like(st_scr)

    C = GLA_CHUNK
    tm = q_ref.shape[0]
    r = lax.broadcasted_iota(I32, (C, C), 0)
    c = lax.broadcasted_iota(I32, (C, C), 1)
    lower = c <= r
    tril = jnp.where(lower, 1.0, 0.0).astype(BF16)

    a_hi, a_mid, _ = _split3(a_ref[...])
    w_hi, w_mid, _ = _split3(wa_ref[...])
    gate_x = _dot(a_hi, w_hi) + (_dot(a_hi, w_mid) + _dot(a_mid, w_hi)) + ba_ref[...]
    log_a = jax.nn.log_sigmoid(gate_x) / GLA_GATE_TAU

    st = st_scr[...]
    for ci in range(tm // C):
        rows = slice(ci * C, (ci + 1) * C)
        la_hi, la_mid, la_lo = _split3(log_a[rows])
        g = _dot(tril, la_hi) + _dot(tril, la_mid) + _dot(tril, la_lo)
        q = q_ref[rows, :]
        k = k_ref[rows, :]
        v = v_ref[rows, :].astype(BF16)
        g_mid = g[C // 2:C // 2 + 1, :]
        g_last = g[C - 1:C, :]
        inter = _dot_nt((q * jnp.exp(g)).astype(BF16), st.astype(BF16))
        qr = (q * jnp.exp(jnp.minimum(g - g_mid, 80.0))).astype(BF16)
        kr = (k * jnp.exp(jnp.minimum(g_mid - g, 80.0))).astype(BF16)
        att = jnp.where(lower, _dot_nt(qr, kr), 0.0)
        intra = _dot(att.astype(BF16), v)
        kd = (k * jnp.exp(g_last - g)).astype(BF16)
        st = st * jnp.exp(g_last) + _dot_tn(v, kd)
        go = inter + intra
        y = go * lax.rsqrt(jnp.mean(go * go, axis=-1, keepdims=True) + RMS_EPS) * gn_ref[...]
        o_ref[rows, :] = y * jax.nn.silu(r_ref[rows, :])
    st_scr[...] = st


def _gla(z, wa2_pad, ba, gnorm):
    S = z.shape[0]
    tm = 256
    hb = lambda col: (lambda h, i: (i, col // LANE + h))
    return pl.pallas_call(
        _gla_kernel,
        grid=(GLA_HEADS, S // tm),
        in_specs=[pl.BlockSpec((tm, LANE), hb(COL_GQ)), pl.BlockSpec((tm, LANE), hb(COL_GK)),
                  pl.BlockSpec((tm, LANE), hb(COL_GV)), pl.BlockSpec((tm, LANE), hb(COL_GR)),
                  pl.BlockSpec((tm, LANE), lambda h, i: (i, COL_GA // LANE)),
                  pl.BlockSpec((LANE, LANE), lambda h, i: (0, h)),
                  pl.BlockSpec((1, LANE), lambda h, i: (0, h)),
                  pl.BlockSpec((1, LANE), lambda h, i: (0, 0))],
        out_specs=pl.BlockSpec((tm, LANE), lambda h, i: (i, h)),
        out_shape=jax.ShapeDtypeStruct((S, GLA_W), F32),
        scratch_shapes=[pltpu.VMEM((HEAD_DIM, HEAD_DIM), F32)],
        compiler_params=_cparams(("arbitrary", "arbitrary")),
        name="gla_chunked",
    )(z, z, z, z, z, wa2_pad, ba, gnorm)


def _sort_key(x):
    u = pltpu.bitcast(x, I32)
    return u ^ ((u >> 31) & 0x7FFFFFFF)


def _indexer_kernel(iq_ref, ik_ref, iw_ref, o_ref, key_scr, *, tq, topk):
    i = pl.program_id(0)
    S = ik_ref.shape[0]
    nk = S // tq
    wt = iw_ref[...].T[0:IDX_HEADS, :] * (IDX_HEADS ** -0.5 * IDX_DIM ** -0.5)
    qrys = i * tq + lax.broadcasted_iota(I32, (tq, tq), 1)
    key0 = lax.broadcasted_iota(I32, (tq, tq), 0)

    def score_chunk(jc, carry):
        off = pl.multiple_of(jc * tq, tq)
        ikb = ik_ref[pl.ds(off, tq), :]
        acc = jnp.zeros((tq, tq), F32)
        for h in range(IDX_HEADS):
            qh = iq_ref[:, h * LANE:(h + 1) * LANE]
            acc = acc + jnp.maximum(_dot_nt(ikb, qh), 0.0) * wt[h:h + 1, :]
        key_scr[pl.ds(off, tq), :] = jnp.where(off + key0 <= qrys, _sort_key(acc), INT_MIN)
        return carry

    lax.fori_loop(0, i + 1, score_chunk, 0)

    def bisect(it, thr):
        cand = thr + lax.shift_left(jnp.int32(1), 31 - it)

        def count_chunk(jc, cnt):
            off = pl.multiple_of(jc * tq, tq)
            ge = jnp.where(key_scr[pl.ds(off, tq), :] >= cand, 1, 0)
            return cnt + jnp.sum(ge.reshape(tq // SUBLANE, SUBLANE, tq), axis=0)

        cnt = lax.fori_loop(0, i + 1, count_chunk, jnp.zeros((SUBLANE, tq), I32))
        return jnp.where(jnp.sum(cnt, axis=0, keepdims=True) >= topk, cand, thr)

    thr = lax.fori_loop(0, 32, bisect, jnp.full((1, tq), INT_MIN, I32))
    thr = jnp.maximum(thr, INT_MIN + 1)

    def write_chunk(jc, carry):
        off = pl.multiple_of(jc * tq, tq)
        keep = key_scr[pl.ds(off, tq), :] >= thr
        o_ref[pl.ds(off, tq), :] = jnp.where(keep, 0.0, NEG).astype(BF16)
        return carry

    lax.fori_loop(0, i + 1, write_chunk, 0)

    def fill_chunk(jc, carry):
        off = pl.multiple_of(jc * tq, tq)
        o_ref[pl.ds(off, tq), :] = jnp.full((tq, tq), NEG, BF16)
        return carry

    lax.fori_loop(i + 1, nk, fill_chunk, 0)


def _dsa_mask(z, zb):
    S = z.shape[0]
    tq = 256
    topk = min(DSA_MAX_TOPK, S // 4)
    return pl.pallas_call(
        functools.partial(_indexer_kernel, tq=tq, topk=topk),
        grid=(S // tq,),
        in_specs=[pl.BlockSpec((tq, IDX_HEADS * LANE), lambda i: (i, COL_IQ // (IDX_HEADS * LANE))),
                  pl.BlockSpec((S, LANE), lambda i: (0, COL_IK // LANE)),
                  pl.BlockSpec((tq, LANE), lambda i: (i, COL_IW // LANE))],
        out_specs=pl.BlockSpec((S, tq), lambda i: (0, i)),
        out_shape=jax.ShapeDtypeStruct((S, S), BF16),
        scratch_shapes=[pltpu.VMEM((S, tq), I32)],
        compiler_params=_cparams(("arbitrary",)),
        name="dsa_indexer_mask",
    )(zb, zb, z)


def _outproj_kernel(fox_ref, gla_ref, dsa_ref, w_ref, x_ref, g1_ref, n2_ref, sc_ref, sh_ref,
                    xo_ref, h_ref):
    acc = _dot(fox_ref[...].astype(BF16), w_ref[0:FOX_W, :])
    acc = acc + _dot(gla_ref[...].astype(BF16), w_ref[FOX_W:FOX_W + GLA_W, :])
    acc = acc + _dot(dsa_ref[...].astype(BF16), w_ref[FOX_W + GLA_W:, :])
    xn = x_ref[...] + g1_ref[...] * acc
    xo_ref[...] = xn
    h_ref[...] = _ada_norm(xn, n2_ref[...], sc_ref[...], sh_ref[...]).astype(BF16)


def _outproj(fox, gla, dsa, w_out, x, g1, n2, sc2, sh2):
    S, D = x.shape
    tm = 256
    vec = pl.BlockSpec((1, D), lambda i: (0, 0))
    row = lambda w: pl.BlockSpec((tm, w), lambda i: (i, 0))
    return pl.pallas_call(
        _outproj_kernel,
        grid=(S // tm,),
        in_specs=[row(FOX_W), row(GLA_W), row(DSA_W), pl.BlockSpec((D, D), lambda i: (0, 0)),
                  row(D), vec, vec, vec, vec],
        out_specs=[row(D), row(D)],
        out_shape=[jax.ShapeDtypeStruct((S, D), F32), jax.ShapeDtypeStruct((S, D), BF16)],
        compiler_params=_cparams(("arbitrary",)),
        name="outproj_residual_norm",
    )(fox, gla, dsa, w_out, x, g1, n2, sc2, sh2)


def _top_rows(x, n):
    rows = x.shape[0]
    ridx = lax.broadcasted_iota(I32, x.shape, 0)
    rank = jnp.full(x.shape, float(n), F32)
    vals = []
    for k in range(n):
        m = jnp.max(x, axis=0, keepdims=True)
        first = jnp.min(jnp.where(x == m, ridx, rows), axis=0, keepdims=True)
        sel = ridx == first
        x = jnp.where(sel, -jnp.inf, x)
        rank = jnp.where(sel, float(k), rank)
        vals.append(m)
    return jnp.concatenate(vals, axis=0), rank


def _peer_score_kernel(h_ref, wq_ref, k1_ref, k2_ref, n_ref, e1_ref, r2_ref, e2_ref, q_scr):
    q = _dot(h_ref[...], wq_ref[...])
    for h in range(PEER_HEADS):
        q_scr[h] = q[:, h * PEER_DQ:(h + 1) * PEER_DQ].astype(BF16)
    half = PEER_DQ // 2
    K = PEER_TOPK

    for h in range(PEER_HEADS):
        qh = q_scr[h]
        s1 = _dot_nt(k1_ref[h], qh[:, :half])
        s2 = _dot_nt(k2_ref[h], qh[:, half:])
        v1, r1 = _top_rows(s1, K)
        v2, r2 = _top_rows(s2, K)
        groups = ([v1[0:1, :] + v2] + [v1[a:a + 1, :] + v2[0:SUBLANE, :] for a in range(1, SUBLANE)]
                  + [v1[SUBLANE:K, :] + v2[0:1, :]])
        best, _ = _top_rows(jnp.concatenate(groups, axis=0), K)
        tau = best[K - 1:K, :]
        zsum = jnp.sum(jnp.exp(best - best[0:1, :]), axis=0, keepdims=True)
        n = jnp.zeros(s1.shape, F32)
        for a in range(SUBLANE):
            n_a = jnp.sum(jnp.where(groups[a] >= tau, 1.0, 0.0), axis=0, keepdims=True)
            n = jnp.where(r1 == float(a), n_a, n)
        tail = jnp.where(groups[SUBLANE] >= tau, 1.0, 0.0)
        for a in range(SUBLANE, K):
            n = jnp.where(r1 == float(a), tail[a - SUBLANE:a - SUBLANE + 1, :], n)
        n_ref[h] = n
        e1_ref[h] = jnp.where(r1 < float(K), jnp.exp(s1 - v1[0:1, :]), 0.0)
        r2_ref[h] = r2.astype(BF16)
        e2_ref[h] = (jnp.where(r2 < float(K), jnp.exp(s2 - v2[0:1, :]), 0.0) / zsum).astype(BF16)


def _peer_scores(h2, wq, k1, k2):
    S, D = h2.shape
    tm = 256
    big = pl.BlockSpec((PEER_HEADS, PEER_NKEYS, tm), lambda i: (0, 0, i))
    kspec = pl.BlockSpec((PEER_HEADS, PEER_NKEYS, PEER_DQ // 2), lambda i: (0, 0, 0))
    f32s = jax.ShapeDtypeStruct((PEER_HEADS, PEER_NKEYS, S), F32)
    b16s = jax.ShapeDtypeStruct((PEER_HEADS, PEER_NKEYS, S), BF16)
    return pl.pallas_call(
        _peer_score_kernel,
        grid=(S // tm,),
        in_specs=[pl.BlockSpec((tm, D), lambda i: (i, 0)),
                  pl.BlockSpec((D, PEER_HEADS * PEER_DQ), lambda i: (0, 0)), kspec, kspec],
        out_specs=[big, big, big, big],
        out_shape=[f32s, f32s, b16s, b16s],
        scratch_shapes=[pltpu.VMEM((PEER_HEADS, tm, PEER_DQ), BF16)],
        compiler_params=_cparams(("arbitrary",)),
        name="peer_scores",
    )(h2, wq, k1, k2)


def _peer_mix_kernel(h_ref, u_ref, vta_ref, vtb_ref, n_ref, e1_ref, r2_ref, e2_ref, x_ref, g2_ref,
                     o_ref, acc_scr, p_scr, w_scr, *, eb, nb):
    s = pl.program_id(1)
    last = pl.num_programs(1) - 1
    ng = eb // PEER_NKEYS
    zero = jnp.zeros((), BF16)

    @pl.when(s == 0)
    def _():
        acc_scr[...] = jnp.zeros_like(acc_scr)
        p_scr[1] = jnp.zeros(p_scr.shape[1:], F32)

    def produce(slot):
        p_scr[slot] = _dot_nt(u_ref[slot * eb:(slot + 1) * eb, :], h_ref[...])

    def consume(slot, blk, vt_ref):
        for cc in range(ng):
            rows = slice(cc * PEER_NKEYS, (cc + 1) * PEER_NKEYS)
            c = blk * ng + cc
            g = None
            for h in range(PEER_HEADS):
                keep = r2_ref[h] < n_ref[h, pl.ds(c, 1), :].astype(BF16)
                gh = jnp.where(keep, e2_ref[h], zero) * e1_ref[h, pl.ds(c, 1), :].astype(BF16)
                g = gh if g is None else g + gh
            w_scr[slot, rows, :] = g * jax.nn.gelu(p_scr[slot, rows, :]).astype(BF16)
        acc_scr[...] += _dot(vt_ref[...], w_scr[slot])

    produce(0)
    consume(1, jnp.maximum(2 * s - 1, 0), vta_ref)

    @pl.when(s < last)
    def _():
        produce(1)
        consume(0, jnp.minimum(2 * s, nb - 1), vtb_ref)

    @pl.when(s == last)
    def _():
        o_ref[...] = x_ref[...] + g2_ref[...] * acc_scr[...].T


def _peer_mix(h2, u, vt, n, e1, r2, e2, x, g2):
    S, D = x.shape
    tm, eb = 512, 512
    nb = PEER_EXPERTS // eb
    once = dict(pipeline_mode=pl.Buffered(1))
    big = pl.BlockSpec((PEER_HEADS, PEER_NKEYS, tm), lambda i, s: (0, 0, i), **once)
    row = pl.BlockSpec((tm, D), lambda i, s: (i, 0), **once)
    return pl.pallas_call(
        functools.partial(_peer_mix_kernel, eb=eb, nb=nb),
        grid=(S // tm, nb // 2 + 1),
        in_specs=[row,
                  pl.BlockSpec((2 * eb, D), lambda i, s: (jnp.minimum(s, nb // 2 - 1), 0)),
                  pl.BlockSpec((D, eb), lambda i, s: (0, jnp.maximum(2 * s - 1, 0))),
                  pl.BlockSpec((D, eb), lambda i, s: (0, jnp.minimum(2 * s, nb - 1))),
                  big, big, big, big, row, pl.BlockSpec((1, D), lambda i, s: (0, 0))],
        out_specs=pl.BlockSpec((tm, D), lambda i, s: (i, 0)),
        out_shape=jax.ShapeDtypeStruct((S, D), F32),
        scratch_shapes=[pltpu.VMEM((D, tm), F32), pltpu.VMEM((2, eb, tm), F32),
                        pltpu.VMEM((2, eb, tm), BF16)],
        compiler_params=_cparams(("arbitrary", "arbitrary")),
        name="peer_dense_mix",
    )(h2, u, vt, vt, n, e1, r2, e2, x, g2)


def _final_norm_kernel(x_ref, g_ref, o_ref):
    x = x_ref[...]
    o_ref[...] = x * lax.rsqrt(jnp.mean(x * x, axis=-1, keepdims=True) + RMS_EPS) * g_ref[...]


def _final_norm(x, g):
    S, D = x.shape
    tm = 512
    return pl.pallas_call(
        _final_norm_kernel,
        grid=(S // tm,),
        in_specs=[pl.BlockSpec((tm, D), lambda i: (i, 0)), pl.BlockSpec((1, D), lambda i: (0, 0))],
        out_specs=pl.BlockSpec((tm, D), lambda i: (i, 0)),
        out_shape=jax.ShapeDtypeStruct((S, D), F32),
        compiler_params=_cparams(("arbitrary",)),
        name="final_rmsnorm",
    )(x, g)


def kernel(x, c, ada_w, ada_b, norm1_g, norm2_g, final_g, w_in, fox_fbias, gla_wa2, gla_ba,
           gla_norm_g, w_out, peer_wq, peer_k1, peer_k2, peer_u, peer_v):
    B, S, D = x.shape
    assert B == 1 and D == D_MODEL
    L = ada_w.shape[0]
    xs = x.reshape(S, D)
    mod = _modulation(c, ada_w, ada_b)
    for l in range(L):
        sh1, sc1, g1, sh2, sc2, g2 = [mod[l, :, m * D:(m + 1) * D] for m in range(N_MOD)]
        z, zb = _inproj(xs, norm1_g[l].reshape(1, D), sc1, sh1, _layout_w_in(w_in, l))

        fcol, frow = _forget_cumsum(z, jnp.pad(fox_fbias[l], (0, LANE - FOX_HEADS)).reshape(1, LANE))
        fox = _fox_attention(z, zb, fcol, frow)

        wa2_pad = jnp.pad(gla_wa2[l], ((0, LANE - GLA_GATE_RANK), (0, 0)))
        gla = _gla(z, wa2_pad, gla_ba[l].reshape(1, GLA_W), gla_norm_g[l].reshape(1, HEAD_DIM))

        dsa = _dsa_attention(z, zb, _dsa_mask(z, zb))

        xs, h2 = _outproj(fox, gla, dsa, _cast_layer(w_out, l, "w_out_cast"), xs, g1,
                          norm2_g[l].reshape(1, D), sc2, sh2)

        n, e1, r2, e2 = _peer_scores(h2, _cast_layer(peer_wq, l, "peer_wq_cast"),
                                     peer_k1[l].astype(BF16), peer_k2[l].astype(BF16))
        vt = _transpose_cast(peer_v, l, PEER_EXPERTS, D, 0, 512, 512, "peer_v_transpose")
        xs = _peer_mix(h2, _cast_layer(peer_u, l, "peer_u_cast"), vt, n, e1, r2, e2, xs, g2)
    return _final_norm(xs, final_g.reshape(1, D)).reshape(B, S, D)
```

```python
import functools

import jax
import jax.numpy as jnp
from jax import lax
from jax.experimental import pallas as pl
from jax.experimental.pallas import tpu as pltpu

F32 = jnp.float32
BF16 = jnp.bfloat16
I32 = jnp.int32

D_MODEL = 2048
HEAD_DIM = 128
FOX_HEADS = 6
GLA_HEADS = 4
DSA_HEADS = 6
FOX_W = FOX_HEADS * HEAD_DIM
GLA_W = GLA_HEADS * HEAD_DIM
DSA_W = DSA_HEADS * HEAD_DIM
GLA_GATE_RANK = 16
GLA_GATE_TAU = 16.0
GLA_CHUNK = 64
IDX_HEADS = 16
IDX_DIM = 64
DSA_MAX_TOPK = 256
PEER_HEADS = 8
PEER_DQ = 256
PEER_NKEYS = 128
PEER_EXPERTS = PEER_NKEYS * PEER_NKEYS
PEER_TOPK = 16
RMS_EPS = 1e-6
N_MOD = 6
N_IN = 3 * FOX_W + FOX_HEADS + 4 * GLA_W + GLA_GATE_RANK + 3 * DSA_W + IDX_HEADS * IDX_DIM + IDX_DIM + IDX_HEADS

LANE = 128
SUBLANE = 8
NEG = -1e30
INT_MIN = -(2 ** 31)
VMEM_LIMIT = 56 * 1024 * 1024
ATTN_SCALE = HEAD_DIM ** -0.5

COL_IQ = 0
COL_GQ = 2048
COL_GK = 2560
COL_GV = 3072
COL_GR = 3584
COL_FF = 4096
COL_GA = 4224
COL_IK = 4352
COL_IW = 4480
COL_FQ = 4608
COL_FK = 5376
COL_FV = 6144
COL_DQ = 6912
COL_DK = 7680
COL_DV = 8448
N_IN_PAD = 9216


def _w_in_groups():
    sizes = (("fq", FOX_W), ("fk", FOX_W), ("fv", FOX_W), ("ff", FOX_HEADS), ("gq", GLA_W),
             ("gk", GLA_W), ("gv", GLA_W), ("gr", GLA_W), ("ga", GLA_GATE_RANK), ("dq", DSA_W),
             ("dk", DSA_W), ("dv", DSA_W), ("iq", IDX_HEADS * IDX_DIM), ("ik", IDX_DIM),
             ("iw", IDX_HEADS))
    dst = dict(fq=COL_FQ, fk=COL_FK, fv=COL_FV, ff=COL_FF, gq=COL_GQ, gk=COL_GK, gv=COL_GV,
               gr=COL_GR, ga=COL_GA, dq=COL_DQ, dk=COL_DK, dv=COL_DV, ik=COL_IK, iw=COL_IW)
    scale = dict(fq=ATTN_SCALE, dq=ATTN_SCALE, gq=ATTN_SCALE)
    groups, o = [], 0
    for name, n in sizes:
        if name == "iq":
            for h in range(IDX_HEADS):
                groups.append((o + h * IDX_DIM, IDX_DIM, COL_IQ + h * LANE, 1.0))
        else:
            groups.append((o, n, dst[name], scale.get(name, 1.0)))
        o += n
    assert o == N_IN
    return tuple(groups)


def _cparams(sem):
    return pltpu.CompilerParams(dimension_semantics=sem, vmem_limit_bytes=VMEM_LIMIT)


def _split3(x):
    hi = x.astype(BF16)
    r1 = x - hi.astype(F32)
    mid = r1.astype(BF16)
    lo = (r1 - mid.astype(F32)).astype(BF16)
    return hi, mid, lo


def _dot(a, b):
    return jnp.dot(a, b, preferred_element_type=F32)


def _dot_nt(a, b):
    return lax.dot_general(a, b, (((1,), (1,)), ((), ())), preferred_element_type=F32)


def _dot_tn(a, b):
    return lax.dot_general(a, b, (((0,), (0,)), ((), ())), preferred_element_type=F32)


def _mod_kernel(c_ref, w_ref, b_ref, o_ref):
    c = c_ref[...]
    ca = jax.nn.silu(c)
    o_ref[0] = jnp.sum(w_ref[0] * ca, axis=0, keepdims=True) + b_ref[0]


def _modulation(c, ada_w, ada_b):
    L, D, E = ada_w.shape
    tn = 1024
    return pl.pallas_call(
        _mod_kernel,
        grid=(L, E // tn),
        in_specs=[pl.BlockSpec((D, 1), lambda l, j: (0, 0)),
                  pl.BlockSpec((1, D, tn), lambda l, j: (l, 0, j)),
                  pl.BlockSpec((1, 1, tn), lambda l, j: (l, 0, j))],
        out_specs=pl.BlockSpec((1, 1, tn), lambda l, j: (l, 0, j)),
        out_shape=jax.ShapeDtypeStruct((L, 1, E), F32),
        compiler_params=_cparams(("arbitrary", "arbitrary")),
        name="adaln_mod",
    )(c.reshape(D, 1), ada_w, ada_b.reshape(L, 1, E))


def _relayout_kernel(w_ref, o_ref):
    o_ref[...] = jnp.zeros_like(o_ref)
    for src, n, dst, scale in _w_in_groups():
        w = w_ref[:, src:src + n]
        if scale != 1.0:
            w = w * scale
        o_ref[:, dst:dst + n] = w.astype(BF16)


def _layout_w_in(w_in, l):
    _, D, N = w_in.shape
    tr = 256
    return pl.pallas_call(
        _relayout_kernel,
        grid=(D // tr,),
        in_specs=[pl.BlockSpec((None, tr, N), lambda i: (l, i, 0))],
        out_specs=pl.BlockSpec((tr, N_IN_PAD), lambda i: (i, 0)),
        out_shape=jax.ShapeDtypeStruct((D, N_IN_PAD), BF16),
        compiler_params=_cparams(("arbitrary",)),
        name="w_in_relayout",
    )(w_in)


def _cast_kernel(x_ref, o_ref):
    o_ref[...] = x_ref[...].astype(BF16)


def _cast_layer(x, layer, name):
    _, R, C = x.shape
    tr = 512
    return pl.pallas_call(
        _cast_kernel,
        grid=(R // tr,),
        in_specs=[pl.BlockSpec((None, tr, C), lambda i: (layer, i, 0))],
        out_specs=pl.BlockSpec((tr, C), lambda i: (i, 0)),
        out_shape=jax.ShapeDtypeStruct((R, C), BF16),
        compiler_params=_cparams(("arbitrary",)),
        name=name,
    )(x)


def _transpose_cast_kernel(x_ref, o_ref):
    o_ref[...] = x_ref[...].T.astype(BF16)


def _transpose_cast(x, layer, rows, cols, col0, tr, tc, name, blocked=False):
    if layer is None:
        spec = pl.BlockSpec((tr, tc), lambda i, j: (i, col0 // tc + j))
    else:
        spec = pl.BlockSpec((None, tr, tc), lambda i, j: (layer, i, col0 // tc + j))
    if blocked:
        return pl.pallas_call(
            _transpose_cast_kernel,
            grid=(rows // tr, cols // tc),
            in_specs=[spec],
            out_specs=pl.BlockSpec((None, tc, tr), lambda i, j: (i, j, 0)),
            out_shape=jax.ShapeDtypeStruct((rows // tr, cols, tr), BF16),
            compiler_params=_cparams(("arbitrary", "arbitrary")),
            name=name,
        )(x)
    return pl.pallas_call(
        _transpose_cast_kernel,
        grid=(rows // tr, cols // tc),
        in_specs=[spec],
        out_specs=pl.BlockSpec((tc, tr), lambda i, j: (j, i)),
        out_shape=jax.ShapeDtypeStruct((cols, rows), BF16),
        compiler_params=_cparams(("arbitrary", "arbitrary")),
        name=name,
    )(x)


def _ada_norm(x, g, sc, sh):
    y = x * lax.rsqrt(jnp.mean(x * x, axis=-1, keepdims=True) + RMS_EPS) * g
    return y * (1.0 + sc) + sh


def _inproj_kernel(x_ref, g_ref, sc_ref, sh_ref, w_ref, o_ref, ob_ref, h_scr):
    @pl.when(pl.program_id(1) == 0)
    def _():
        h_scr[...] = _ada_norm(x_ref[...], g_ref[...], sc_ref[...], sh_ref[...]).astype(BF16)

    z = _dot(h_scr[...], w_ref[...])
    o_ref[...] = z
    ob_ref[...] = z.astype(BF16)


def _inproj(x, g, sc, sh, w_p):
    S, D = x.shape
    N = w_p.shape[1]
    tm, tn = 1024, 512
    vec = pl.BlockSpec((1, D), lambda i, j: (0, 0))
    out = pl.BlockSpec((tm, tn), lambda i, j: (i, j))
    return pl.pallas_call(
        _inproj_kernel,
        grid=(S // tm, N // tn),
        in_specs=[pl.BlockSpec((tm, D), lambda i, j: (i, 0)), vec, vec, vec,
                  pl.BlockSpec((D, tn), lambda i, j: (0, j))],
        out_specs=[out, out],
        out_shape=[jax.ShapeDtypeStruct((S, N), F32), jax.ShapeDtypeStruct((S, N), BF16)],
        scratch_shapes=[pltpu.VMEM((tm, D), BF16)],
        compiler_params=_cparams(("arbitrary", "arbitrary")),
        name="norm_inproj",
    )(x, g, sc, sh, w_p)


def _forget_kernel(ff_ref, fb_ref, fcol_ref, frow_ref, carry):
    @pl.when(pl.program_id(0) == 0)
    def _():
        carry[...] = jnp.zeros_like(carry)

    tm = ff_ref.shape[0]
    logf = jax.nn.log_sigmoid(ff_ref[...] + fb_ref[...])
    r = lax.broadcasted_iota(I32, (tm, tm), 0)
    c = lax.broadcasted_iota(I32, (tm, tm), 1)
    tril = jnp.where(c <= r, 1.0, 0.0).astype(BF16)
    hi, mid, lo = _split3(logf)
    cum = _dot(tril, hi) + _dot(tril, mid) + _dot(tril, lo) + carry[...]
    carry[...] = cum[tm - 1:tm, :]
    fcol_ref[...] = cum
    frow_ref[...] = cum.T[0:SUBLANE, :]


def _forget_cumsum(z, fbias_pad):
    S = z.shape[0]
    tm = 256
    return pl.pallas_call(
        _forget_kernel,
        grid=(S // tm,),
        in_specs=[pl.BlockSpec((tm, LANE), lambda i: (i, COL_FF // LANE)),
                  pl.BlockSpec((1, LANE), lambda i: (0, 0))],
        out_specs=[pl.BlockSpec((tm, LANE), lambda i: (i, 0)),
                   pl.BlockSpec((SUBLANE, tm), lambda i: (0, i))],
        out_shape=[jax.ShapeDtypeStruct((S, LANE), F32), jax.ShapeDtypeStruct((SUBLANE, S), F32)],
        scratch_shapes=[pltpu.VMEM((1, LANE), F32)],
        compiler_params=_cparams(("arbitrary",)),
        name="fox_forget_cumsum",
    )(z, fbias_pad)


def _flash_kernel(*refs, fox, nh, tq, tk):
    if fox:
        q_ref, k_ref, vt_ref, fq_ref, fk_ref, o_ref, m_scr, l_scr, acc_scr, s_scr, p_scr = refs
    else:
        q_ref, k_ref, vt_ref, b_ref, o_ref, m_scr, l_scr, acc_scr, s_scr, p_scr = refs
    i = pl.program_id(0)
    j = pl.program_id(1)

    @pl.when(j == 0)
    def _():
        m_scr[...] = jnp.full_like(m_scr, NEG)
        l_scr[...] = jnp.zeros_like(l_scr)
        acc_scr[...] = jnp.zeros_like(acc_scr)

    def step(diagonal):
        for h in range(nh):
            sl = slice(h * HEAD_DIM, (h + 1) * HEAD_DIM)
            s_scr[h] = _dot_nt(k_ref[:, sl], q_ref[:, sl])
        if fox:
            if diagonal:
                keys = j * tk + lax.broadcasted_iota(I32, (tk, tq), 0)
                qrys = i * tq + lax.broadcasted_iota(I32, (tk, tq), 1)
                causal = keys <= qrys
        else:
            bias = b_ref[...].astype(F32)
        m_all = m_scr[...]
        l_all = l_scr[...]
        m_rows, l_rows, alphas = [], [], []
        for h in range(nh):
            s = s_scr[h]
            if fox:
                s = s + (fq_ref[h:h + 1, :] - fk_ref[:, h:h + 1])
                if diagonal:
                    s = jnp.where(causal, s, NEG)
            else:
                s = s + bias
            m_prev = m_all[h:h + 1, :]
            m_new = jnp.maximum(m_prev, jnp.max(s, axis=0, keepdims=True))
            alpha = jnp.exp(m_prev - m_new)
            p = jnp.exp(s - m_new)
            l_rows.append(alpha * l_all[h:h + 1, :] + jnp.sum(p, axis=0, keepdims=True))
            m_rows.append(m_new)
            alphas.append(alpha)
            p_scr[h] = p.astype(BF16)
        m_scr[...] = jnp.concatenate(m_rows + [m_all[nh:, :]], axis=0)
        l_scr[...] = jnp.concatenate(l_rows + [l_all[nh:, :]], axis=0)
        for h in range(nh):
            sl = slice(h * HEAD_DIM, (h + 1) * HEAD_DIM)
            acc_scr[sl, :] = alphas[h] * acc_scr[sl, :] + _dot(vt_ref[sl, :], p_scr[h])

    r = tq // tk
    j_last = r * i + (r - 1)
    if fox:
        pl.when(j < r * i)(lambda: step(False))
        pl.when(jnp.logical_and(j >= r * i, j <= j_last))(lambda: step(True))
    else:
        pl.when(j <= j_last)(lambda: step(False))

    @pl.when(j == j_last)
    def _():
        for h in range(nh):
            sl = slice(h * HEAD_DIM, (h + 1) * HEAD_DIM)
            o_ref[:, sl] = (acc_scr[sl, :] / l_scr[h:h + 1, :]).T


def _flash(zb, vt, col_q, col_k, nh, *, fcol=None, frow=None, bias_t=None):
    S = zb.shape[0]
    W = nh * HEAD_DIM
    tq, tk = 512, 512
    fox = bias_t is None
    kj = lambda i, j: jnp.minimum(j, (tq // tk) * i + (tq // tk - 1))
    qspec = pl.BlockSpec((tq, W), lambda i, j: (i, col_q // W))
    kspec = pl.BlockSpec((tk, W), lambda i, j: (kj(i, j), col_k // W))
    vspec = pl.BlockSpec((W, tk), lambda i, j: (0, kj(i, j)))
    if fox:
        extra = [frow, fcol]
        especs = [pl.BlockSpec((SUBLANE, tq), lambda i, j: (0, i)),
                  pl.BlockSpec((tk, LANE), lambda i, j: (kj(i, j), 0))]
    else:
        extra = [bias_t]
        especs = [pl.BlockSpec((tk, tq), lambda i, j: (kj(i, j), i))]
    return pl.pallas_call(
        functools.partial(_flash_kernel, fox=fox, nh=nh, tq=tq, tk=tk),
        grid=(S // tq, S // tk),
        in_specs=[qspec, kspec, vspec] + especs,
        out_specs=pl.BlockSpec((tq, W), lambda i, j: (i, 0)),
        out_shape=jax.ShapeDtypeStruct((S, W), F32),
        scratch_shapes=[pltpu.VMEM((SUBLANE, tq), F32), pltpu.VMEM((SUBLANE, tq), F32),
                        pltpu.VMEM((W, tq), F32), pltpu.VMEM((nh, tk, tq), F32),
                        pltpu.VMEM((nh, tk, tq), BF16)],
        compiler_params=_cparams(("arbitrary", "arbitrary")),
        name="fox_flash" if fox else "dsa_flash",
    )(zb, zb, vt, *extra)


def _fox_attention(z, zb, fcol, frow):
    vt = _transpose_cast(z, None, z.shape[0], FOX_W, COL_FV, 256, 256, "fox_v_transpose")
    return _flash(zb, vt, COL_FQ, COL_FK, FOX_HEADS, fcol=fcol, frow=frow)


def _dsa_attention(z, zb, bias_t):
    vt = _transpose_cast(z, None, z.shape[0], DSA_W, COL_DV, 256, 256, "dsa_v_transpose")
    return _flash(zb, vt, COL_DQ, COL_DK, DSA_HEADS, bias_t=bias_t)


def _gla_kernel(q_ref, k_ref, v_ref, r_ref, a_ref, wa_ref, ba_ref, gn_ref, o_ref, st_scr):
    @pl.when(pl.program_id(1) == 0)
    def _():
        st_scr[...] = jnp.zeros_like(st_scr)

    C = GLA_CHUNK
    tm = q_ref.shape[0]
    r = lax.broadcasted_iota(I32, (C, C), 0)
    c = lax.broadcasted_iota(I32, (C, C), 1)
    lower = c <= r
    tril = jnp.where(lower, 1.0, 0.0).astype(BF16)

    a_hi, a_mid, _ = _split3(a_ref[...])
    w_hi, w_mid, _ = _split3(wa_ref[...])
    gate_x = _dot(a_hi, w_hi) + (_dot(a_hi, w_mid) + _dot(a_mid, w_hi)) + ba_ref[...]
    log_a = jax.nn.log_sigmoid(gate_x) / GLA_GATE_TAU

    st = st_scr[...]
    for ci in range(tm // C):
        rows = slice(ci * C, (ci + 1) * C)
        la_hi, la_mid, la_lo = _split3(log_a[rows])
        g = _dot(tril, la_hi) + _dot(tril, la_mid) + _dot(tril, la_lo)
        q = q_ref[rows, :]
        k = k_ref[rows, :]
        v = v_ref[rows, :].astype(BF16)
        g_mid = g[C // 2:C // 2 + 1, :]
        g_last = g[C - 1:C, :]
        inter = _dot_nt((q * jnp.exp(g)).astype(BF16), st.astype(BF16))
        qr = (q * jnp.exp(jnp.minimum(g - g_mid, 80.0))).astype(BF16)
        kr = (k * jnp.exp(jnp.minimum(g_mid - g, 80.0))).astype(BF16)
        att = jnp.where(lower, _dot_nt(qr, kr), 0.0)
        intra = _dot(att.astype(BF16), v)
        kd = (k * jnp.exp(g_last - g)).astype(BF16)
        st = st * jnp.exp(g_last) + _dot_tn(v, kd)
        go = inter + intra
        y = go * lax.rsqrt(jnp.mean(go * go, axis=-1, keepdims=True) + RMS_EPS) * gn_ref[...]
        o_ref[rows, :] = y * jax.nn.silu(r_ref[rows, :])
    st_scr[...] = st


def _gla(z, wa2_pad, ba, gnorm):
    S = z.shape[0]
    tm = 256
    hb = lambda col: (lambda h, i: (i, col // LANE + h))
    return pl.pallas_call(
        _gla_kernel,
        grid=(GLA_HEADS, S // tm),
        in_specs=[pl.BlockSpec((tm, LANE), hb(COL_GQ)), pl.BlockSpec((tm, LANE), hb(COL_GK)),
                  pl.BlockSpec((tm, LANE), hb(COL_GV)), pl.BlockSpec((tm, LANE), hb(COL_GR)),
                  pl.BlockSpec((tm, LANE), lambda h, i: (i, COL_GA // LANE)),
                  pl.BlockSpec((LANE, LANE), lambda h, i: (0, h)),
                  pl.BlockSpec((1, LANE), lambda h, i: (0, h)),
                  pl.BlockSpec((1, LANE), lambda h, i: (0, 0))],
        out_specs=pl.BlockSpec((tm, LANE), lambda h, i: (i, h)),
        out_shape=jax.ShapeDtypeStruct((S, GLA_W), F32),
        scratch_shapes=[pltpu.VMEM((HEAD_DIM, HEAD_DIM), F32)],
        compiler_params=_cparams(("arbitrary", "arbitrary")),
        name="gla_chunked",
    )(z, z, z, z, z, wa2_pad, ba, gnorm)


def _sort_key(x):
    u = pltpu.bitcast(x, I32)
    return u ^ ((u >> 31) & 0x7FFFFFFF)


def _indexer_kernel(iq_ref, ik_ref, iw_ref, o_ref, key_scr, *, tq, topk):
    i = pl.program_id(0)
    S = ik_ref.shape[0]
    nk = S // tq
    wt = iw_ref[...].T[0:IDX_HEADS, :] * (IDX_HEADS ** -0.5 * IDX_DIM ** -0.5)
    qrys = i * tq + lax.broadcasted_iota(I32, (tq, tq), 1)
    key0 = lax.broadcasted_iota(I32, (tq, tq), 0)

    def score_chunk(jc, carry):
        off = pl.multiple_of(jc * tq, tq)
        ikb = ik_ref[pl.ds(off, tq), :]
        acc = jnp.zeros((tq, tq), F32)
        for h in range(IDX_HEADS):
            qh = iq_ref[:, h * LANE:(h + 1) * LANE]
            acc = acc + jnp.maximum(_dot_nt(ikb, qh), 0.0) * wt[h:h + 1, :]
        key_scr[pl.ds(off, tq), :] = jnp.where(off + key0 <= qrys, _sort_key(acc), INT_MIN)
        return carry

    lax.fori_loop(0, i + 1, score_chunk, 0)

    def bisect(it, thr):
        cand = thr + lax.shift_left(jnp.int32(1), 31 - it)

        def count_chunk(jc, cnt):
            off = pl.multiple_of(jc * tq, tq)
            ge = jnp.where(key_scr[pl.ds(off, tq), :] >= cand, 1, 0)
            return cnt + jnp.sum(ge.reshape(tq // SUBLANE, SUBLANE, tq), axis=0)

        cnt = lax.fori_loop(0, i + 1, count_chunk, jnp.zeros((SUBLANE, tq), I32))
        return jnp.where(jnp.sum(cnt, axis=0, keepdims=True) >= topk, cand, thr)

    thr = lax.fori_loop(0, 32, bisect, jnp.full((1, tq), INT_MIN, I32))
    thr = jnp.maximum(thr, INT_MIN + 1)

    def write_chunk(jc, carry):
        off = pl.multiple_of(jc * tq, tq)
        keep = key_scr[pl.ds(off, tq), :] >= thr
        o_ref[pl.ds(off, tq), :] = jnp.where(keep, 0.0, NEG).astype(BF16)
        return carry

    lax.fori_loop(0, i + 1, write_chunk, 0)

    def fill_chunk(jc, carry):
        off = pl.multiple_of(jc * tq, tq)
        o_ref[pl.ds(off, tq), :] = jnp.full((tq, tq), NEG, BF16)
        return carry

    lax.fori_loop(i + 1, nk, fill_chunk, 0)


def _dsa_mask(z, zb):
    S = z.shape[0]
    tq = 256
    topk = min(DSA_MAX_TOPK, S // 4)
    return pl.pallas_call(
        functools.partial(_indexer_kernel, tq=tq, topk=topk),
        grid=(S // tq,),
        in_specs=[pl.BlockSpec((tq, IDX_HEADS * LANE), lambda i: (i, COL_IQ // (IDX_HEADS * LANE))),
                  pl.BlockSpec((S, LANE), lambda i: (0, COL_IK // LANE)),
                  pl.BlockSpec((tq, LANE), lambda i: (i, COL_IW // LANE))],
        out_specs=pl.BlockSpec((S, tq), lambda i: (0, i)),
        out_shape=jax.ShapeDtypeStruct((S, S), BF16),
        scratch_shapes=[pltpu.VMEM((S, tq), I32)],
        compiler_params=_cparams(("arbitrary",)),
        name="dsa_indexer_mask",
    )(zb, zb, z)


def _outproj_kernel(fox_ref, gla_ref, dsa_ref, w_ref, x_ref, g1_ref, n2_ref, sc_ref, sh_ref,
                    xo_ref, h_ref):
    acc = _dot(fox_ref[...].astype(BF16), w_ref[0:FOX_W, :])
    acc = acc + _dot(gla_ref[...].astype(BF16), w_ref[FOX_W:FOX_W + GLA_W, :])
    acc = acc + _dot(dsa_ref[...].astype(BF16), w_ref[FOX_W + GLA_W:, :])
    xn = x_ref[...] + g1_ref[...] * acc
    xo_ref[...] = xn
    h_ref[...] = _ada_norm(xn, n2_ref[...], sc_ref[...], sh_ref[...]).astype(BF16)


def _outproj(fox, gla, dsa, w_out, x, g1, n2, sc2, sh2):
    S, D = x.shape
    tm = 256
    vec = pl.BlockSpec((1, D), lambda i: (0, 0))
    row = lambda w: pl.BlockSpec((tm, w), lambda i: (i, 0))
    return pl.pallas_call(
        _outproj_kernel,
        grid=(S // tm,),
        in_specs=[row(FOX_W), row(GLA_W), row(DSA_W), pl.BlockSpec((D, D), lambda i: (0, 0)),
                  row(D), vec, vec, vec, vec],
        out_specs=[row(D), row(D)],
        out_shape=[jax.ShapeDtypeStruct((S, D), F32), jax.ShapeDtypeStruct((S, D), BF16)],
        compiler_params=_cparams(("arbitrary",)),
        name="outproj_residual_norm",
    )(fox, gla, dsa, w_out, x, g1, n2, sc2, sh2)


def _top_rows(x, n):
    rows = x.shape[0]
    ridx = lax.broadcasted_iota(I32, x.shape, 0)
    rank = jnp.full(x.shape, float(n), F32)
    vals = []
    for k in range(n):
        m = jnp.max(x, axis=0, keepdims=True)
        first = jnp.min(jnp.where(x == m, ridx, rows), axis=0, keepdims=True)
        sel = ridx == first
        x = jnp.where(sel, -jnp.inf, x)
        rank = jnp.where(sel, float(k), rank)
        vals.append(m)
    return jnp.concatenate(vals, axis=0), rank


def _peer_score_kernel(h_ref, wq_ref, k1_ref, k2_ref, n_ref, e1_ref, r2_ref, e2_ref, q_scr):
    q = _dot(h_ref[...], wq_ref[...])
    for h in range(PEER_HEADS):
        q_scr[h] = q[:, h * PEER_DQ:(h + 1) * PEER_DQ].astype(BF16)
    half = PEER_DQ // 2
    K = PEER_TOPK

    for h in range(PEER_HEADS):
        qh = q_scr[h]
        s1 = _dot_nt(k1_ref[h], qh[:, :half])
        s2 = _dot_nt(k2_ref[h], qh[:, half:])
        v1, r1 = _top_rows(s1, K)
        v2, r2 = _top_rows(s2, K)
        groups = ([v1[0:1, :] + v2] + [v1[a:a + 1, :] + v2[0:SUBLANE, :] for a in range(1, SUBLANE)]
                  + [v1[SUBLANE:K, :] + v2[0:1, :]])
        best, _ = _top_rows(jnp.concatenate(groups, axis=0), K)
        tau = best[K - 1:K, :]
        zsum = jnp.sum(jnp.exp(best - best[0:1, :]), axis=0, keepdims=True)
        n = jnp.zeros(s1.shape, F32)
        for a in range(SUBLANE):
            n_a = jnp.sum(jnp.where(groups[a] >= tau, 1.0, 0.0), axis=0, keepdims=True)
            n = jnp.where(r1 == float(a), n_a, n)
        tail = jnp.where(groups[SUBLANE] >= tau, 1.0, 0.0)
        for a in range(SUBLANE, K):
            n = jnp.where(r1 == float(a), tail[a - SUBLANE:a - SUBLANE + 1, :], n)
        n_ref[h] = n
        e1_ref[h] = jnp.where(r1 < float(K), jnp.exp(s1 - v1[0:1, :]), 0.0)
        r2_ref[h] = r2.astype(BF16)
        e2_ref[h] = (jnp.where(r2 < float(K), jnp.exp(s2 - v2[0:1, :]), 0.0) / zsum).astype(BF16)


def _peer_scores(h2, wq, k1, k2):
    S, D = h2.shape
    tm = 256
    big = pl.BlockSpec((PEER_HEADS, PEER_NKEYS, tm), lambda i: (0, 0, i))
    kspec = pl.BlockSpec((PEER_HEADS, PEER_NKEYS, PEER_DQ // 2), lambda i: (0, 0, 0))
    f32s = jax.ShapeDtypeStruct((PEER_HEADS, PEER_NKEYS, S), F32)
    b16s = jax.ShapeDtypeStruct((PEER_HEADS, PEER_NKEYS, S), BF16)
    return pl.pallas_call(
        _peer_score_kernel,
        grid=(S // tm,),
        in_specs=[pl.BlockSpec((tm, D), lambda i: (i, 0)),
                  pl.BlockSpec((D, PEER_HEADS * PEER_DQ), lambda i: (0, 0)), kspec, kspec],
        out_specs=[big, big, big, big],
        out_shape=[f32s, f32s, b16s, b16s],
        scratch_shapes=[pltpu.VMEM((PEER_HEADS, tm, PEER_DQ), BF16)],
        compiler_params=_cparams(("arbitrary",)),
        name="peer_scores",
    )(h2, wq, k1, k2)


def _peer_mix_kernel(h_ref, u_ref, vta_ref, vtb_ref, n_ref, e1_ref, r2_ref, e2_ref, x_ref, g2_ref,
                     o_ref, acc_scr, p_scr, w_scr, *, eb, nb):
    s = pl.program_id(1)
    last = pl.num_programs(1) - 1
    ng = eb // PEER_NKEYS
    zero = jnp.zeros((), BF16)

    @pl.when(s == 0)
    def _():
        acc_scr[...] = jnp.zeros_like(acc_scr)
        p_scr[1] = jnp.zeros(p_scr.shape[1:], F32)

    def produce(slot):
        p_scr[slot] = _dot_nt(u_ref[slot * eb:(slot + 1) * eb, :], h_ref[...])

    def consume(slot, blk, vt_ref):
        for cc in range(ng):
            rows = slice(cc * PEER_NKEYS, (cc + 1) * PEER_NKEYS)
            c = blk * ng + cc
            g = None
            for h in range(PEER_HEADS):
                keep = r2_ref[h] < n_ref[h, pl.ds(c, 1), :].astype(BF16)
                gh = jnp.where(keep, e2_ref[h], zero) * e1_ref[h, pl.ds(c, 1), :].astype(BF16)
                g = gh if g is None else g + gh
            w_scr[slot, rows, :] = g * jax.nn.gelu(p_scr[slot, rows, :]).astype(BF16)
        acc_scr[...] += _dot(vt_ref[...], w_scr[slot])

    produce(0)
    consume(1, jnp.maximum(2 * s - 1, 0), vta_ref)

    @pl.when(s < last)
    def _():
        produce(1)
        consume(0, jnp.minimum(2 * s, nb - 1), vtb_ref)

    @pl.when(s == last)
    def _():
        o_ref[...] = x_ref[...] + g2_ref[...] * acc_scr[...].T


def _peer_mix(h2, u, vt, n, e1, r2, e2, x, g2):
    S, D = x.shape
    tm, eb = 512, 512
    nb = PEER_EXPERTS // eb
    once = dict(pipeline_mode=pl.Buffered(1))
    big = pl.BlockSpec((PEER_HEADS, PEER_NKEYS, tm), lambda i, s: (0, 0, i), **once)
    row = pl.BlockSpec((tm, D), lambda i, s: (i, 0), **once)
    return pl.pallas_call(
        functools.partial(_peer_mix_kernel, eb=eb, nb=nb),
        grid=(S // tm, nb // 2 + 1),
        in_specs=[row,
                  pl.BlockSpec((2 * eb, D), lambda i, s: (jnp.minimum(s, nb // 2 - 1), 0)),
                  pl.BlockSpec((None, D, eb), lambda i, s: (jnp.maximum(2 * s - 1, 0), 0, 0)),
                  pl.BlockSpec((None, D, eb), lambda i, s: (jnp.minimum(2 * s, nb - 1), 0, 0)),
                  big, big, big, big, row, pl.BlockSpec((1, D), lambda i, s: (0, 0))],
        out_specs=pl.BlockSpec((tm, D), lambda i, s: (i, 0)),
        out_shape=jax.ShapeDtypeStruct((S, D), F32),
        scratch_shapes=[pltpu.VMEM((D, tm), F32), pltpu.VMEM((2, eb, tm), F32),
                        pltpu.VMEM((2, eb, tm), BF16)],
        compiler_params=_cparams(("arbitrary", "arbitrary")),
        name="peer_dense_mix",
    )(h2, u, vt, vt, n, e1, r2, e2, x, g2)


def _final_norm_kernel(x_ref, g_ref, o_ref):
    x = x_ref[...]
    o_ref[...] = x * lax.rsqrt(jnp.mean(x * x, axis=-1, keepdims=True) + RMS_EPS) * g_ref[...]


def _final_norm(x, g):
    S, D = x.shape
    tm = 512
    return pl.pallas_call(
        _final_norm_kernel,
        grid=(S // tm,),
        in_specs=[pl.BlockSpec((tm, D), lambda i: (i, 0)), pl.BlockSpec((1, D), lambda i: (0, 0))],
        out_specs=pl.BlockSpec((tm, D), lambda i: (i, 0)),
        out_shape=jax.ShapeDtypeStruct((S, D), F32),
        compiler_params=_cparams(("arbitrary",)),
        name="final_rmsnorm",
    )(x, g)


def kernel(x, c, ada_w, ada_b, norm1_g, norm2_g, final_g, w_in, fox_fbias, gla_wa2, gla_ba,
           gla_norm_g, w_out, peer_wq, peer_k1, peer_k2, peer_u, peer_v):
    B, S, D = x.shape
    assert B == 1 and D == D_MODEL
    L = ada_w.shape[0]
    xs = x.reshape(S, D)
    mod = _modulation(c, ada_w, ada_b)
    for l in range(L):
        sh1, sc1, g1, sh2, sc2, g2 = [mod[l, :, m * D:(m + 1) * D] for m in range(N_MOD)]
        z, zb = _inproj(xs, norm1_g[l].reshape(1, D), sc1, sh1, _layout_w_in(w_in, l))

        fcol, frow = _forget_cumsum(z, jnp.pad(fox_fbias[l], (0, LANE - FOX_HEADS)).reshape(1, LANE))
        fox = _fox_attention(z, zb, fcol, frow)

        wa2_pad = jnp.pad(gla_wa2[l], ((0, LANE - GLA_GATE_RANK), (0, 0)))
        gla = _gla(z, wa2_pad, gla_ba[l].reshape(1, GLA_W), gla_norm_g[l].reshape(1, HEAD_DIM))

        dsa = _dsa_attention(z, zb, _dsa_mask(z, zb))

        xs, h2 = _outproj(fox, gla, dsa, _cast_layer(w_out, l, "w_out_cast"), xs, g1,
                          norm2_g[l].reshape(1, D), sc2, sh2)

        n, e1, r2, e2 = _peer_scores(h2, _cast_layer(peer_wq, l, "peer_wq_cast"),
                                     peer_k1[l].astype(BF16), peer_k2[l].astype(BF16))
        vt = _transpose_cast(peer_v, l, PEER_EXPERTS, D, 0, 512, 512, "peer_v_transpose",
                             blocked=True)
        xs = _peer_mix(h2, _cast_layer(peer_u, l, "peer_u_cast"), vt, n, e1, r2, e2, xs, g2)
    return _final_norm(xs, final_g.reshape(1, D)).reshape(B, S, D)
```

```python
import functools

import jax
import jax.numpy as jnp
from jax import lax
from jax.experimental import pallas as pl
from jax.experimental.pallas import tpu as pltpu

F32 = jnp.float32
BF16 = jnp.bfloat16
I32 = jnp.int32

D_MODEL = 2048
HEAD_DIM = 128
FOX_HEADS = 6
GLA_HEADS = 4
DSA_HEADS = 6
FOX_W = FOX_HEADS * HEAD_DIM
GLA_W = GLA_HEADS * HEAD_DIM
DSA_W = DSA_HEADS * HEAD_DIM
GLA_GATE_RANK = 16
GLA_GATE_TAU = 16.0
GLA_CHUNK = 64
IDX_HEADS = 16
IDX_DIM = 64
DSA_MAX_TOPK = 256
PEER_HEADS = 8
PEER_DQ = 256
PEER_NKEYS = 128
PEER_EXPERTS = PEER_NKEYS * PEER_NKEYS
PEER_TOPK = 16
RMS_EPS = 1e-6
N_MOD = 6
N_IN = 3 * FOX_W + FOX_HEADS + 4 * GLA_W + GLA_GATE_RANK + 3 * DSA_W + IDX_HEADS * IDX_DIM + IDX_DIM + IDX_HEADS

LANE = 128
SUBLANE = 8
NEG = -1e30
INT_MIN = -(2 ** 31)
VMEM_LIMIT = 56 * 1024 * 1024
ATTN_SCALE = HEAD_DIM ** -0.5

COL_IQ = 0
COL_GQ = 2048
COL_GK = 2560
COL_GV = 3072
COL_GR = 3584
COL_FF = 4096
COL_GA = 4224
COL_IK = 4352
COL_IW = 4480
COL_FQ = 4608
COL_FK = 5376
COL_FV = 6144
COL_DQ = 6912
COL_DK = 7680
COL_DV = 8448
N_IN_PAD = 9216


def _w_in_groups():
    sizes = (("fq", FOX_W), ("fk", FOX_W), ("fv", FOX_W), ("ff", FOX_HEADS), ("gq", GLA_W),
             ("gk", GLA_W), ("gv", GLA_W), ("gr", GLA_W), ("ga", GLA_GATE_RANK), ("dq", DSA_W),
             ("dk", DSA_W), ("dv", DSA_W), ("iq", IDX_HEADS * IDX_DIM), ("ik", IDX_DIM),
             ("iw", IDX_HEADS))
    dst = dict(fq=COL_FQ, fk=COL_FK, fv=COL_FV, ff=COL_FF, gq=COL_GQ, gk=COL_GK, gv=COL_GV,
               gr=COL_GR, ga=COL_GA, dq=COL_DQ, dk=COL_DK, dv=COL_DV, ik=COL_IK, iw=COL_IW)
    scale = dict(fq=ATTN_SCALE, dq=ATTN_SCALE, gq=ATTN_SCALE)
    groups, o = [], 0
    for name, n in sizes:
        if name == "iq":
            for h in range(IDX_HEADS):
                groups.append((o + h * IDX_DIM, IDX_DIM, COL_IQ + h * LANE, 1.0))
        else:
            groups.append((o, n, dst[name], scale.get(name, 1.0)))
        o += n
    assert o == N_IN
    return tuple(groups)


def _cparams(sem):
    return pltpu.CompilerParams(dimension_semantics=sem, vmem_limit_bytes=VMEM_LIMIT)


def _split3(x):
    hi = x.astype(BF16)
    r1 = x - hi.astype(F32)
    mid = r1.astype(BF16)
    lo = (r1 - mid.astype(F32)).astype(BF16)
    return hi, mid, lo


def _dot(a, b):
    return jnp.dot(a, b, preferred_element_type=F32)


def _dot_nt(a, b):
    return lax.dot_general(a, b, (((1,), (1,)), ((), ())), preferred_element_type=F32)


def _dot_tn(a, b):
    return lax.dot_general(a, b, (((0,), (0,)), ((), ())), preferred_element_type=F32)


def _mod_kernel(c_ref, w_ref, b_ref, o_ref):
    c = c_ref[...]
    ca = jax.nn.silu(c)
    o_ref[0] = jnp.sum(w_ref[0] * ca, axis=0, keepdims=True) + b_ref[0]


def _modulation(c, ada_w, ada_b):
    L, D, E = ada_w.shape
    tn = 1024
    return pl.pallas_call(
        _mod_kernel,
        grid=(L, E // tn),
        in_specs=[pl.BlockSpec((D, 1), lambda l, j: (0, 0)),
                  pl.BlockSpec((1, D, tn), lambda l, j: (l, 0, j)),
                  pl.BlockSpec((1, 1, tn), lambda l, j: (l, 0, j))],
        out_specs=pl.BlockSpec((1, 1, tn), lambda l, j: (l, 0, j)),
        out_shape=jax.ShapeDtypeStruct((L, 1, E), F32),
        compiler_params=_cparams(("arbitrary", "arbitrary")),
        name="adaln_mod",
    )(c.reshape(D, 1), ada_w, ada_b.reshape(L, 1, E))


def _relayout_kernel(w_ref, o_ref):
    o_ref[...] = jnp.zeros_like(o_ref)
    for src, n, dst, scale in _w_in_groups():
        w = w_ref[:, src:src + n]
        if scale != 1.0:
            w = w * scale
        o_ref[:, dst:dst + n] = w.astype(BF16)


def _layout_w_in(w_in, l):
    _, D, N = w_in.shape
    tr = 256
    return pl.pallas_call(
        _relayout_kernel,
        grid=(D // tr,),
        in_specs=[pl.BlockSpec((None, tr, N), lambda i: (l, i, 0))],
        out_specs=pl.BlockSpec((tr, N_IN_PAD), lambda i: (i, 0)),
        out_shape=jax.ShapeDtypeStruct((D, N_IN_PAD), BF16),
        compiler_params=_cparams(("arbitrary",)),
        name="w_in_relayout",
    )(w_in)


def _cast_kernel(x_ref, o_ref):
    o_ref[...] = x_ref[...].astype(BF16)


def _cast_layer(x, layer, name):
    _, R, C = x.shape
    tr = 512
    return pl.pallas_call(
        _cast_kernel,
        grid=(R // tr,),
        in_specs=[pl.BlockSpec((None, tr, C), lambda i: (layer, i, 0))],
        out_specs=pl.BlockSpec((tr, C), lambda i: (i, 0)),
        out_shape=jax.ShapeDtypeStruct((R, C), BF16),
        compiler_params=_cparams(("arbitrary",)),
        name=name,
    )(x)


def _transpose_cast_kernel(x_ref, o_ref):
    o_ref[...] = x_ref[...].T.astype(BF16)


def _transpose_cast(x, layer, rows, cols, col0, tr, tc, name, blocked=False):
    if layer is None:
        spec = pl.BlockSpec((tr, tc), lambda i, j: (i, col0 // tc + j))
    else:
        spec = pl.BlockSpec((None, tr, tc), lambda i, j: (layer, i, col0 // tc + j))
    if blocked:
        return pl.pallas_call(
            _transpose_cast_kernel,
            grid=(rows // tr, cols // tc),
            in_specs=[spec],
            out_specs=pl.BlockSpec((None, tc, tr), lambda i, j: (i, j, 0)),
            out_shape=jax.ShapeDtypeStruct((rows // tr, cols, tr), BF16),
            compiler_params=_cparams(("arbitrary", "arbitrary")),
            name=name,
        )(x)
    return pl.pallas_call(
        _transpose_cast_kernel,
        grid=(rows // tr, cols // tc),
        in_specs=[spec],
        out_specs=pl.BlockSpec((tc, tr), lambda i, j: (j, i)),
        out_shape=jax.ShapeDtypeStruct((cols, rows), BF16),
        compiler_params=_cparams(("arbitrary", "arbitrary")),
        name=name,
    )(x)


def _ada_norm(x, g, sc, sh):
    y = x * lax.rsqrt(jnp.mean(x * x, axis=-1, keepdims=True) + RMS_EPS) * g
    return y * (1.0 + sc) + sh


def _inproj_kernel(x_ref, g_ref, sc_ref, sh_ref, w_ref, o_ref, ob_ref, h_scr):
    @pl.when(pl.program_id(1) == 0)
    def _():
        h_scr[...] = _ada_norm(x_ref[...], g_ref[...], sc_ref[...], sh_ref[...]).astype(BF16)

    z = _dot(h_scr[...], w_ref[...])
    o_ref[...] = z
    ob_ref[...] = z.astype(BF16)


def _inproj(x, g, sc, sh, w_p):
    S, D = x.shape
    N = w_p.shape[1]
    tm, tn = 1024, 512
    vec = pl.BlockSpec((1, D), lambda i, j: (0, 0))
    out = pl.BlockSpec((tm, tn), lambda i, j: (i, j))
    return pl.pallas_call(
        _inproj_kernel,
        grid=(S // tm, N // tn),
        in_specs=[pl.BlockSpec((tm, D), lambda i, j: (i, 0)), vec, vec, vec,
                  pl.BlockSpec((D, tn), lambda i, j: (0, j))],
        out_specs=[out, out],
        out_shape=[jax.ShapeDtypeStruct((S, N), F32), jax.ShapeDtypeStruct((S, N), BF16)],
        scratch_shapes=[pltpu.VMEM((tm, D), BF16)],
        compiler_params=_cparams(("arbitrary", "arbitrary")),
        name="norm_inproj",
    )(x, g, sc, sh, w_p)


def _forget_kernel(ff_ref, fb_ref, fcol_ref, frow_ref, carry):
    @pl.when(pl.program_id(0) == 0)
    def _():
        carry[...] = jnp.zeros_like(carry)

    tm = ff_ref.shape[0]
    logf = jax.nn.log_sigmoid(ff_ref[...] + fb_ref[...])
    r = lax.broadcasted_iota(I32, (tm, tm), 0)
    c = lax.broadcasted_iota(I32, (tm, tm), 1)
    tril = jnp.where(c <= r, 1.0, 0.0).astype(BF16)
    hi, mid, lo = _split3(logf)
    cum = _dot(tril, hi) + _dot(tril, mid) + _dot(tril, lo) + carry[...]
    carry[...] = cum[tm - 1:tm, :]
    fcol_ref[...] = cum
    frow_ref[...] = cum.T[0:SUBLANE, :]


def _forget_cumsum(z, fbias_pad):
    S = z.shape[0]
    tm = 256
    return pl.pallas_call(
        _forget_kernel,
        grid=(S // tm,),
        in_specs=[pl.BlockSpec((tm, LANE), lambda i: (i, COL_FF // LANE)),
                  pl.BlockSpec((1, LANE), lambda i: (0, 0))],
        out_specs=[pl.BlockSpec((tm, LANE), lambda i: (i, 0)),
                   pl.BlockSpec((SUBLANE, tm), lambda i: (0, i))],
        out_shape=[jax.ShapeDtypeStruct((S, LANE), F32), jax.ShapeDtypeStruct((SUBLANE, S), F32)],
        scratch_shapes=[pltpu.VMEM((1, LANE), F32)],
        compiler_params=_cparams(("arbitrary",)),
        name="fox_forget_cumsum",
    )(z, fbias_pad)


def _flash_kernel(*refs, fox, nh, tq, tk):
    if fox:
        q_ref, k_ref, vt_ref, fq_ref, fk_ref, o_ref, m_scr, l_scr, acc_scr, s_scr, p_scr = refs
    else:
        q_ref, k_ref, vt_ref, b_ref, o_ref, m_scr, l_scr, acc_scr, s_scr, p_scr = refs
    i = pl.program_id(0)
    j = pl.program_id(1)

    @pl.when(j == 0)
    def _():
        m_scr[...] = jnp.full_like(m_scr, NEG)
        l_scr[...] = jnp.zeros_like(l_scr)
        acc_scr[...] = jnp.zeros_like(acc_scr)

    def step(diagonal):
        for h in range(nh):
            sl = slice(h * HEAD_DIM, (h + 1) * HEAD_DIM)
            s_scr[h] = _dot_nt(k_ref[:, sl], q_ref[:, sl])
        if fox:
            if diagonal:
                keys = j * tk + lax.broadcasted_iota(I32, (tk, tq), 0)
                qrys = i * tq + lax.broadcasted_iota(I32, (tk, tq), 1)
                causal = keys <= qrys
        else:
            bias = b_ref[...].astype(F32)
        m_all = m_scr[...]
        l_all = l_scr[...]
        m_rows, l_rows, alphas = [], [], []
        for h in range(nh):
            s = s_scr[h]
            if fox:
                s = s + (fq_ref[h:h + 1, :] - fk_ref[:, h:h + 1])
                if diagonal:
                    s = jnp.where(causal, s, NEG)
            else:
                s = s + bias
            m_prev = m_all[h:h + 1, :]
            m_new = jnp.maximum(m_prev, jnp.max(s, axis=0, keepdims=True))
            alpha = jnp.exp(m_prev - m_new)
            p = jnp.exp(s - m_new)
            l_rows.append(alpha * l_all[h:h + 1, :] + jnp.sum(p, axis=0, keepdims=True))
            m_rows.append(m_new)
            alphas.append(alpha)
            p_scr[h] = p.astype(BF16)
        m_scr[...] = jnp.concatenate(m_rows + [m_all[nh:, :]], axis=0)
        l_scr[...] = jnp.concatenate(l_rows + [l_all[nh:, :]], axis=0)
        for h in range(nh):
            sl = slice(h * HEAD_DIM, (h + 1) * HEAD_DIM)
            acc_scr[sl, :] = alphas[h] * acc_scr[sl, :] + _dot(vt_ref[sl, :], p_scr[h])

    r = tq // tk
    j_last = r * i + (r - 1)
    if fox:
        pl.when(j < r * i)(lambda: step(False))
        pl.when(jnp.logical_and(j >= r * i, j <= j_last))(lambda: step(True))
    else:
        pl.when(j <= j_last)(lambda: step(False))

    @pl.when(j == j_last)
    def _():
        for h in range(nh):
            sl = slice(h * HEAD_DIM, (h + 1) * HEAD_DIM)
            o_ref[:, sl] = (acc_scr[sl, :] / l_scr[h:h + 1, :]).T


def _flash(zb, vt, col_q, col_k, nh, *, fcol=None, frow=None, bias_t=None):
    S = zb.shape[0]
    W = nh * HEAD_DIM
    tq, tk = 512, 512
    fox = bias_t is None
    kj = lambda i, j: jnp.minimum(j, (tq // tk) * i + (tq // tk - 1))
    qspec = pl.BlockSpec((tq, W), lambda i, j: (i, col_q // W))
    kspec = pl.BlockSpec((tk, W), lambda i, j: (kj(i, j), col_k // W))
    vspec = pl.BlockSpec((W, tk), lambda i, j: (0, kj(i, j)))
    if fox:
        extra = [frow, fcol]
        especs = [pl.BlockSpec((SUBLANE, tq), lambda i, j: (0, i)),
                  pl.BlockSpec((tk, LANE), lambda i, j: (kj(i, j), 0))]
    else:
        extra = [bias_t]
        especs = [pl.BlockSpec((tk, tq), lambda i, j: (kj(i, j), i))]
    return pl.pallas_call(
        functools.partial(_flash_kernel, fox=fox, nh=nh, tq=tq, tk=tk),
        grid=(S // tq, S // tk),
        in_specs=[qspec, kspec, vspec] + especs,
        out_specs=pl.BlockSpec((tq, W), lambda i, j: (i, 0)),
        out_shape=jax.ShapeDtypeStruct((S, W), F32),
        scratch_shapes=[pltpu.VMEM((SUBLANE, tq), F32), pltpu.VMEM((SUBLANE, tq), F32),
                        pltpu.VMEM((W, tq), F32), pltpu.VMEM((nh, tk, tq), F32),
                        pltpu.VMEM((nh, tk, tq), BF16)],
        compiler_params=_cparams(("arbitrary", "arbitrary")),
        name="fox_flash" if fox else "dsa_flash",
    )(zb, zb, vt, *extra)


def _fox_attention(z, zb, fcol, frow):
    vt = _transpose_cast(z, None, z.shape[0], FOX_W, COL_FV, 256, 256, "fox_v_transpose")
    return _flash(zb, vt, COL_FQ, COL_FK, FOX_HEADS, fcol=fcol, frow=frow)


def _dsa_attention(z, zb, bias_t):
    vt = _transpose_cast(z, None, z.shape[0], DSA_W, COL_DV, 256, 256, "dsa_v_transpose")
    return _flash(zb, vt, COL_DQ, COL_DK, DSA_HEADS, bias_t=bias_t)


def _gla_kernel(q_ref, k_ref, v_ref, r_ref, a_ref, wa_ref, ba_ref, gn_ref, o_ref, st_scr):
    @pl.when(pl.program_id(0) == 0)
    def _():
        st_scr[...] = jnp.zeros_like(st_scr)

    C = GLA_CHUNK
    tm = q_ref.shape[0]
    r = lax.broadcasted_iota(I32, (C, C), 0)
    c = lax.broadcasted_iota(I32, (C, C), 1)
    lower = c <= r
    tril = jnp.where(lower, 1.0, 0.0).astype(BF16)

    a_hi, a_mid, _ = _split3(a_ref[...])
    w_hi, w_mid, _ = _split3(wa_ref[...])
    gate_x = _dot(a_hi, w_hi) + (_dot(a_hi, w_mid) + _dot(a_mid, w_hi)) + ba_ref[...]
    log_a = jax.nn.log_sigmoid(gate_x) / GLA_GATE_TAU

    for hd in range(GLA_HEADS):
        cols = slice(hd * HEAD_DIM, (hd + 1) * HEAD_DIM)
        st = st_scr[hd]
        for ci in range(tm // C):
            rows = slice(ci * C, (ci + 1) * C)
            la_hi, la_mid, la_lo = _split3(log_a[rows, cols])
            g = _dot(tril, la_hi) + _dot(tril, la_mid) + _dot(tril, la_lo)
            q = q_ref[rows, cols]
            k = k_ref[rows, cols]
            v = v_ref[rows, cols].astype(BF16)
            g_mid = g[C // 2:C // 2 + 1, :]
            g_last = g[C - 1:C, :]
            inter = _dot_nt((q * jnp.exp(g)).astype(BF16), st.astype(BF16))
            qr = (q * jnp.exp(jnp.minimum(g - g_mid, 80.0))).astype(BF16)
            kr = (k * jnp.exp(jnp.minimum(g_mid - g, 80.0))).astype(BF16)
            att = jnp.where(lower, _dot_nt(qr, kr), 0.0)
            intra = _dot(att.astype(BF16), v)
            kd = (k * jnp.exp(g_last - g)).astype(BF16)
            st = st * jnp.exp(g_last) + _dot_tn(v, kd)
            go = inter + intra
            y = go * lax.rsqrt(jnp.mean(go * go, axis=-1, keepdims=True) + RMS_EPS) * gn_ref[...]
            o_ref[rows, cols] = y * jax.nn.silu(r_ref[rows, cols])
        st_scr[hd] = st


def _gla(z, wa2_pad, ba, gnorm):
    S = z.shape[0]
    tm = 256
    grp = lambda col: pl.BlockSpec((tm, GLA_W), lambda i: (i, col // GLA_W))
    return pl.pallas_call(
        _gla_kernel,
        grid=(S // tm,),
        in_specs=[grp(COL_GQ), grp(COL_GK), grp(COL_GV), grp(COL_GR),
                  pl.BlockSpec((tm, LANE), lambda i: (i, COL_GA // LANE)),
                  pl.BlockSpec((LANE, GLA_W), lambda i: (0, 0)),
                  pl.BlockSpec((1, GLA_W), lambda i: (0, 0)),
                  pl.BlockSpec((1, LANE), lambda i: (0, 0))],
        out_specs=pl.BlockSpec((tm, GLA_W), lambda i: (i, 0)),
        out_shape=jax.ShapeDtypeStruct((S, GLA_W), F32),
        scratch_shapes=[pltpu.VMEM((GLA_HEADS, HEAD_DIM, HEAD_DIM), F32)],
        compiler_params=_cparams(("arbitrary",)),
        name="gla_chunked",
    )(z, z, z, z, z, wa2_pad, ba, gnorm)


def _sort_key(x):
    u = pltpu.bitcast(x, I32)
    return u ^ ((u >> 31) & 0x7FFFFFFF)


def _indexer_kernel(iq_ref, ik_ref, iw_ref, o_ref, key_scr, *, tq, topk):
    i = pl.program_id(0)
    S = ik_ref.shape[0]
    nk = S // tq
    wt = iw_ref[...].T[0:IDX_HEADS, :] * (IDX_HEADS ** -0.5 * IDX_DIM ** -0.5)
    qrys = i * tq + lax.broadcasted_iota(I32, (tq, tq), 1)
    key0 = lax.broadcasted_iota(I32, (tq, tq), 0)

    def score_chunk(jc, carry):
        off = pl.multiple_of(jc * tq, tq)
        ikb = ik_ref[pl.ds(off, tq), :]
        acc = jnp.zeros((tq, tq), F32)
        for h in range(IDX_HEADS):
            qh = iq_ref[:, h * LANE:(h + 1) * LANE]
            acc = acc + jnp.maximum(_dot_nt(ikb, qh), 0.0) * wt[h:h + 1, :]
        key_scr[pl.ds(off, tq), :] = jnp.where(off + key0 <= qrys, _sort_key(acc), INT_MIN)
        return carry

    lax.fori_loop(0, i + 1, score_chunk, 0)

    def bisect(it, thr):
        cand = thr + lax.shift_left(jnp.int32(1), 31 - it)

        def count_chunk(jc, cnt):
            off = pl.multiple_of(jc * tq, tq)
            ge = jnp.where(key_scr[pl.ds(off, tq), :] >= cand, 1, 0)
            return cnt + jnp.sum(ge.reshape(tq // SUBLANE, SUBLANE, tq), axis=0)

        cnt = lax.fori_loop(0, i + 1, count_chunk, jnp.zeros((SUBLANE, tq), I32))
        return jnp.where(jnp.sum(cnt, axis=0, keepdims=True) >= topk, cand, thr)

    thr = lax.fori_loop(0, 32, bisect, jnp.full((1, tq), INT_MIN, I32))
    thr = jnp.maximum(thr, INT_MIN + 1)

    def write_chunk(jc, carry):
        off = pl.multiple_of(jc * tq, tq)
        keep = key_scr[pl.ds(off, tq), :] >= thr
        o_ref[pl.ds(off, tq), :] = jnp.where(keep, 0.0, NEG).astype(BF16)
        return carry

    lax.fori_loop(0, i + 1, write_chunk, 0)

    def fill_chunk(jc, carry):
        off = pl.multiple_of(jc * tq, tq)
        o_ref[pl.ds(off, tq), :] = jnp.full((tq, tq), NEG, BF16)
        return carry

    lax.fori_loop(i + 1, nk, fill_chunk, 0)


def _dsa_mask(z, zb):
    S = z.shape[0]
    tq = 256
    topk = min(DSA_MAX_TOPK, S // 4)
    return pl.pallas_call(
        functools.partial(_indexer_kernel, tq=tq, topk=topk),
        grid=(S // tq,),
        in_specs=[pl.BlockSpec((tq, IDX_HEADS * LANE), lambda i: (i, COL_IQ // (IDX_HEADS * LANE))),
                  pl.BlockSpec((S, LANE), lambda i: (0, COL_IK // LANE)),
                  pl.BlockSpec((tq, LANE), lambda i: (i, COL_IW // LANE))],
        out_specs=pl.BlockSpec((S, tq), lambda i: (0, i)),
        out_shape=jax.ShapeDtypeStruct((S, S), BF16),
        scratch_shapes=[pltpu.VMEM((S, tq), I32)],
        compiler_params=_cparams(("arbitrary",)),
        name="dsa_indexer_mask",
    )(zb, zb, z)


def _outproj_kernel(fox_ref, gla_ref, dsa_ref, w_ref, x_ref, g1_ref, n2_ref, sc_ref, sh_ref,
                    xo_ref, h_ref):
    acc = _dot(fox_ref[...].astype(BF16), w_ref[0:FOX_W, :])
    acc = acc + _dot(gla_ref[...].astype(BF16), w_ref[FOX_W:FOX_W + GLA_W, :])
    acc = acc + _dot(dsa_ref[...].astype(BF16), w_ref[FOX_W + GLA_W:, :])
    xn = x_ref[...] + g1_ref[...] * acc
    xo_ref[...] = xn
    h_ref[...] = _ada_norm(xn, n2_ref[...], sc_ref[...], sh_ref[...]).astype(BF16)


def _outproj(fox, gla, dsa, w_out, x, g1, n2, sc2, sh2):
    S, D = x.shape
    tm = 256
    vec = pl.BlockSpec((1, D), lambda i: (0, 0))
    row = lambda w: pl.BlockSpec((tm, w), lambda i: (i, 0))
    return pl.pallas_call(
        _outproj_kernel,
        grid=(S // tm,),
        in_specs=[row(FOX_W), row(GLA_W), row(DSA_W), pl.BlockSpec((D, D), lambda i: (0, 0)),
                  row(D), vec, vec, vec, vec],
        out_specs=[row(D), row(D)],
        out_shape=[jax.ShapeDtypeStruct((S, D), F32), jax.ShapeDtypeStruct((S, D), BF16)],
        compiler_params=_cparams(("arbitrary",)),
        name="outproj_residual_norm",
    )(fox, gla, dsa, w_out, x, g1, n2, sc2, sh2)


def _top_rows(x, n):
    rows = x.shape[0]
    ridx = lax.broadcasted_iota(I32, x.shape, 0)
    rank = jnp.full(x.shape, float(n), F32)
    vals = []
    for k in range(n):
        m = jnp.max(x, axis=0, keepdims=True)
        first = jnp.min(jnp.where(x == m, ridx, rows), axis=0, keepdims=True)
        sel = ridx == first
        x = jnp.where(sel, -jnp.inf, x)
        rank = jnp.where(sel, float(k), rank)
        vals.append(m)
    return jnp.concatenate(vals, axis=0), rank


def _peer_score_kernel(h_ref, wq_ref, k1_ref, k2_ref, n_ref, e1_ref, r2_ref, e2_ref, q_scr):
    q = _dot(h_ref[...], wq_ref[...])
    for h in range(PEER_HEADS):
        q_scr[h] = q[:, h * PEER_DQ:(h + 1) * PEER_DQ].astype(BF16)
    half = PEER_DQ // 2
    K = PEER_TOPK

    for h in range(PEER_HEADS):
        qh = q_scr[h]
        s1 = _dot_nt(k1_ref[h], qh[:, :half])
        s2 = _dot_nt(k2_ref[h], qh[:, half:])
        v1, r1 = _top_rows(s1, K)
        v2, r2 = _top_rows(s2, K)
        groups = ([v1[0:1, :] + v2] + [v1[a:a + 1, :] + v2[0:SUBLANE, :] for a in range(1, SUBLANE)]
                  + [v1[SUBLANE:K, :] + v2[0:1, :]])
        best, _ = _top_rows(jnp.concatenate(groups, axis=0), K)
        tau = best[K - 1:K, :]
        zsum = jnp.sum(jnp.exp(best - best[0:1, :]), axis=0, keepdims=True)
        n = jnp.zeros(s1.shape, F32)
        for a in range(SUBLANE):
            n_a = jnp.sum(jnp.where(groups[a] >= tau, 1.0, 0.0), axis=0, keepdims=True)
            n = jnp.where(r1 == float(a), n_a, n)
        tail = jnp.where(groups[SUBLANE] >= tau, 1.0, 0.0)
        for a in range(SUBLANE, K):
            n = jnp.where(r1 == float(a), tail[a - SUBLANE:a - SUBLANE + 1, :], n)
        n_ref[h] = n
        e1_ref[h] = jnp.where(r1 < float(K), jnp.exp(s1 - v1[0:1, :]), 0.0)
        r2_ref[h] = r2.astype(BF16)
        e2_ref[h] = (jnp.where(r2 < float(K), jnp.exp(s2 - v2[0:1, :]), 0.0) / zsum).astype(BF16)


def _peer_scores(h2, wq, k1, k2):
    S, D = h2.shape
    tm = 256
    big = pl.BlockSpec((PEER_HEADS, PEER_NKEYS, tm), lambda i: (0, 0, i))
    kspec = pl.BlockSpec((PEER_HEADS, PEER_NKEYS, PEER_DQ // 2), lambda i: (0, 0, 0))
    f32s = jax.ShapeDtypeStruct((PEER_HEADS, PEER_NKEYS, S), F32)
    b16s = jax.ShapeDtypeStruct((PEER_HEADS, PEER_NKEYS, S), BF16)
    return pl.pallas_call(
        _peer_score_kernel,
        grid=(S // tm,),
        in_specs=[pl.BlockSpec((tm, D), lambda i: (i, 0)),
                  pl.BlockSpec((D, PEER_HEADS * PEER_DQ), lambda i: (0, 0)), kspec, kspec],
        out_specs=[big, big, big, big],
        out_shape=[f32s, f32s, b16s, b16s],
        scratch_shapes=[pltpu.VMEM((PEER_HEADS, tm, PEER_DQ), BF16)],
        compiler_params=_cparams(("arbitrary",)),
        name="peer_scores",
    )(h2, wq, k1, k2)


def _peer_mix_kernel(h_ref, u_ref, vta_ref, vtb_ref, n_ref, e1_ref, r2_ref, e2_ref, x_ref, g2_ref,
                     o_ref, acc_scr, p_scr, w_scr, *, eb, nb):
    s = pl.program_id(1)
    last = pl.num_programs(1) - 1
    ng = eb // PEER_NKEYS
    zero = jnp.zeros((), BF16)

    @pl.when(s == 0)
    def _():
        acc_scr[...] = jnp.zeros_like(acc_scr)
        p_scr[1] = jnp.zeros(p_scr.shape[1:], F32)

    def produce(slot):
        p_scr[slot] = _dot_nt(u_ref[slot * eb:(slot + 1) * eb, :], h_ref[...])

    def consume(slot, blk, vt_ref):
        for cc in range(ng):
            rows = slice(cc * PEER_NKEYS, (cc + 1) * PEER_NKEYS)
            c = blk * ng + cc
            g = None
            for h in range(PEER_HEADS):
                keep = r2_ref[h] < n_ref[h, pl.ds(c, 1), :].astype(BF16)
                gh = jnp.where(keep, e2_ref[h], zero) * e1_ref[h, pl.ds(c, 1), :].astype(BF16)
                g = gh if g is None else g + gh
            w_scr[slot, rows, :] = g * jax.nn.gelu(p_scr[slot, rows, :]).astype(BF16)
        acc_scr[...] += _dot(vt_ref[...], w_scr[slot])

    produce(0)
    consume(1, jnp.maximum(2 * s - 1, 0), vta_ref)

    @pl.when(s < last)
    def _():
        produce(1)
        consume(0, jnp.minimum(2 * s, nb - 1), vtb_ref)

    @pl.when(s == last)
    def _():
        o_ref[...] = x_ref[...] + g2_ref[...] * acc_scr[...].T


def _peer_mix(h2, u, vt, n, e1, r2, e2, x, g2):
    S, D = x.shape
    tm, eb = 512, 512
    nb = PEER_EXPERTS // eb
    once = dict(pipeline_mode=pl.Buffered(1))
    big = pl.BlockSpec((PEER_HEADS, PEER_NKEYS, tm), lambda i, s: (0, 0, i), **once)
    row = pl.BlockSpec((tm, D), lambda i, s: (i, 0), **once)
    return pl.pallas_call(
        functools.partial(_peer_mix_kernel, eb=eb, nb=nb),
        grid=(S // tm, nb // 2 + 1),
        in_specs=[row,
                  pl.BlockSpec((2 * eb, D), lambda i, s: (jnp.minimum(s, nb // 2 - 1), 0)),
                  pl.BlockSpec((None, D, eb), lambda i, s: (jnp.maximum(2 * s - 1, 0), 0, 0)),
                  pl.BlockSpec((None, D, eb), lambda i, s: (jnp.minimum(2 * s, nb - 1), 0, 0)),
                  big, big, big, big, row, pl.BlockSpec((1, D), lambda i, s: (0, 0))],
        out_specs=pl.BlockSpec((tm, D), lambda i, s: (i, 0)),
        out_shape=jax.ShapeDtypeStruct((S, D), F32),
        scratch_shapes=[pltpu.VMEM((D, tm), F32), pltpu.VMEM((2, eb, tm), F32),
                        pltpu.VMEM((2, eb, tm), BF16)],
        compiler_params=_cparams(("arbitrary", "arbitrary")),
        name="peer_dense_mix",
    )(h2, u, vt, vt, n, e1, r2, e2, x, g2)


def _final_norm_kernel(x_ref, g_ref, o_ref):
    x = x_ref[...]
    o_ref[...] = x * lax.rsqrt(jnp.mean(x * x, axis=-1, keepdims=True) + RMS_EPS) * g_ref[...]


def _final_norm(x, g):
    S, D = x.shape
    tm = 512
    return pl.pallas_call(
        _final_norm_kernel,
        grid=(S // tm,),
        in_specs=[pl.BlockSpec((tm, D), lambda i: (i, 0)), pl.BlockSpec((1, D), lambda i: (0, 0))],
        out_specs=pl.BlockSpec((tm, D), lambda i: (i, 0)),
        out_shape=jax.ShapeDtypeStruct((S, D), F32),
        compiler_params=_cparams(("arbitrary",)),
        name="final_rmsnorm",
    )(x, g)


def kernel(x, c, ada_w, ada_b, norm1_g, norm2_g, final_g, w_in, fox_fbias, gla_wa2, gla_ba,
           gla_norm_g, w_out, peer_wq, peer_k1, peer_k2, peer_u, peer_v):
    B, S, D = x.shape
    assert B == 1 and D == D_MODEL
    L = ada_w.shape[0]
    xs = x.reshape(S, D)
    mod = _modulation(c, ada_w, ada_b)
    for l in range(L):
        sh1, sc1, g1, sh2, sc2, g2 = [mod[l, :, m * D:(m + 1) * D] for m in range(N_MOD)]
        z, zb = _inproj(xs, norm1_g[l].reshape(1, D), sc1, sh1, _layout_w_in(w_in, l))

        fcol, frow = _forget_cumsum(z, jnp.pad(fox_fbias[l], (0, LANE - FOX_HEADS)).reshape(1, LANE))
        fox = _fox_attention(z, zb, fcol, frow)

        wa2_pad = jnp.pad(gla_wa2[l], ((0, LANE - GLA_GATE_RANK), (0, 0)))
        gla = _gla(z, wa2_pad, gla_ba[l].reshape(1, GLA_W), gla_norm_g[l].reshape(1, HEAD_DIM))

        dsa = _dsa_attention(z, zb, _dsa_mask(z, zb))

        xs, h2 = _outproj(fox, gla, dsa, _cast_layer(w_out, l, "w_out_cast"), xs, g1,
                          norm2_g[l].reshape(1, D), sc2, sh2)

        n, e1, r2, e2 = _peer_scores(h2, _cast_layer(peer_wq, l, "peer_wq_cast"),
                                     peer_k1[l].astype(BF16), peer_k2[l].astype(BF16))
        vt = _transpose_cast(peer_v, l, PEER_EXPERTS, D, 0, 512, 512, "peer_v_transpose",
                             blocked=True)
        xs = _peer_mix(h2, _cast_layer(peer_u, l, "peer_u_cast"), vt, n, e1, r2, e2, xs, g2)
    return _final_norm(xs, final_g.reshape(1, D)).reshape(B, S, D)
```

```python
import functools

import jax
import jax.numpy as jnp
from jax import lax
from jax.experimental import pallas as pl
from jax.experimental.pallas import tpu as pltpu

F32 = jnp.float32
BF16 = jnp.bfloat16
I32 = jnp.int32

D_MODEL = 2048
HEAD_DIM = 128
FOX_HEADS = 6
GLA_HEADS = 4
DSA_HEADS = 6
FOX_W = FOX_HEADS * HEAD_DIM
GLA_W = GLA_HEADS * HEAD_DIM
DSA_W = DSA_HEADS * HEAD_DIM
GLA_GATE_RANK = 16
GLA_GATE_TAU = 16.0
GLA_CHUNK = 64
IDX_HEADS = 16
IDX_DIM = 64
DSA_MAX_TOPK = 256
PEER_HEADS = 8
PEER_DQ = 256
PEER_NKEYS = 128
PEER_EXPERTS = PEER_NKEYS * PEER_NKEYS
PEER_TOPK = 16
RMS_EPS = 1e-6
N_MOD = 6
N_IN = 3 * FOX_W + FOX_HEADS + 4 * GLA_W + GLA_GATE_RANK + 3 * DSA_W + IDX_HEADS * IDX_DIM + IDX_DIM + IDX_HEADS

LANE = 128
SUBLANE = 8
NEG = -1e30
INT_MIN = -(2 ** 31)
VMEM_LIMIT = 56 * 1024 * 1024
ATTN_SCALE = HEAD_DIM ** -0.5

COL_IQ = 0
COL_GQ = 2048
COL_GK = 2560
COL_GV = 3072
COL_GR = 3584
COL_FF = 4096
COL_GA = 4224
COL_IK = 4352
COL_IW = 4480
COL_FQ = 4608
COL_FK = 5376
COL_FV = 6144
COL_DQ = 6912
COL_DK = 7680
COL_DV = 8448
N_IN_PAD = 9216


def _w_in_groups():
    sizes = (("fq", FOX_W), ("fk", FOX_W), ("fv", FOX_W), ("ff", FOX_HEADS), ("gq", GLA_W),
             ("gk", GLA_W), ("gv", GLA_W), ("gr", GLA_W), ("ga", GLA_GATE_RANK), ("dq", DSA_W),
             ("dk", DSA_W), ("dv", DSA_W), ("iq", IDX_HEADS * IDX_DIM), ("ik", IDX_DIM),
             ("iw", IDX_HEADS))
    dst = dict(fq=COL_FQ, fk=COL_FK, fv=COL_FV, ff=COL_FF, gq=COL_GQ, gk=COL_GK, gv=COL_GV,
               gr=COL_GR, ga=COL_GA, dq=COL_DQ, dk=COL_DK, dv=COL_DV, ik=COL_IK, iw=COL_IW)
    scale = dict(fq=ATTN_SCALE, dq=ATTN_SCALE, gq=ATTN_SCALE)
    groups, o = [], 0
    for name, n in sizes:
        if name == "iq":
            for h in range(IDX_HEADS):
                groups.append((o + h * IDX_DIM, IDX_DIM, COL_IQ + h * LANE, 1.0))
        else:
            groups.append((o, n, dst[name], scale.get(name, 1.0)))
        o += n
    assert o == N_IN
    return tuple(groups)


def _cparams(sem):
    return pltpu.CompilerParams(dimension_semantics=sem, vmem_limit_bytes=VMEM_LIMIT)


def _split3(x):
    hi = x.astype(BF16)
    r1 = x - hi.astype(F32)
    mid = r1.astype(BF16)
    lo = (r1 - mid.astype(F32)).astype(BF16)
    return hi, mid, lo


def _dot(a, b):
    return jnp.dot(a, b, preferred_element_type=F32)


def _dot_nt(a, b):
    return lax.dot_general(a, b, (((1,), (1,)), ((), ())), preferred_element_type=F32)


def _dot_tn(a, b):
    return lax.dot_general(a, b, (((0,), (0,)), ((), ())), preferred_element_type=F32)


def _mod_kernel(c_ref, w_ref, b_ref, o_ref):
    c = c_ref[...]
    ca = jax.nn.silu(c)
    o_ref[0] = jnp.sum(w_ref[0] * ca, axis=0, keepdims=True) + b_ref[0]


def _modulation(c, ada_w, ada_b):
    L, D, E = ada_w.shape
    tn = 1024
    return pl.pallas_call(
        _mod_kernel,
        grid=(L, E // tn),
        in_specs=[pl.BlockSpec((D, 1), lambda l, j: (0, 0)),
                  pl.BlockSpec((1, D, tn), lambda l, j: (l, 0, j)),
                  pl.BlockSpec((1, 1, tn), lambda l, j: (l, 0, j))],
        out_specs=pl.BlockSpec((1, 1, tn), lambda l, j: (l, 0, j)),
        out_shape=jax.ShapeDtypeStruct((L, 1, E), F32),
        compiler_params=_cparams(("arbitrary", "arbitrary")),
        name="adaln_mod",
    )(c.reshape(D, 1), ada_w, ada_b.reshape(L, 1, E))


def _relayout_kernel(w_ref, o_ref):
    o_ref[...] = jnp.zeros_like(o_ref)
    for src, n, dst, scale in _w_in_groups():
        w = w_ref[:, src:src + n]
        if scale != 1.0:
            w = w * scale
        o_ref[:, dst:dst + n] = w.astype(BF16)


def _layout_w_in(w_in, l):
    _, D, N = w_in.shape
    tr = 256
    return pl.pallas_call(
        _relayout_kernel,
        grid=(D // tr,),
        in_specs=[pl.BlockSpec((None, tr, N), lambda i: (l, i, 0))],
        out_specs=pl.BlockSpec((tr, N_IN_PAD), lambda i: (i, 0)),
        out_shape=jax.ShapeDtypeStruct((D, N_IN_PAD), BF16),
        compiler_params=_cparams(("arbitrary",)),
        name="w_in_relayout",
    )(w_in)


def _cast_kernel(x_ref, o_ref):
    o_ref[...] = x_ref[...].astype(BF16)


def _cast_layer(x, layer, name):
    _, R, C = x.shape
    tr = 512
    return pl.pallas_call(
        _cast_kernel,
        grid=(R // tr,),
        in_specs=[pl.BlockSpec((None, tr, C), lambda i: (layer, i, 0))],
        out_specs=pl.BlockSpec((tr, C), lambda i: (i, 0)),
        out_shape=jax.ShapeDtypeStruct((R, C), BF16),
        compiler_params=_cparams(("arbitrary",)),
        name=name,
    )(x)


def _transpose_cast_kernel(x_ref, o_ref):
    o_ref[...] = x_ref[...].T.astype(BF16)


def _transpose_cast(x, layer, rows, cols, col0, tr, tc, name, blocked=False):
    if layer is None:
        spec = pl.BlockSpec((tr, tc), lambda i, j: (i, col0 // tc + j))
    else:
        spec = pl.BlockSpec((None, tr, tc), lambda i, j: (layer, i, col0 // tc + j))
    if blocked:
        return pl.pallas_call(
            _transpose_cast_kernel,
            grid=(rows // tr, cols // tc),
            in_specs=[spec],
            out_specs=pl.BlockSpec((None, tc, tr), lambda i, j: (i, j, 0)),
            out_shape=jax.ShapeDtypeStruct((rows // tr, cols, tr), BF16),
            compiler_params=_cparams(("arbitrary", "arbitrary")),
            name=name,
        )(x)
    return pl.pallas_call(
        _transpose_cast_kernel,
        grid=(rows // tr, cols // tc),
        in_specs=[spec],
        out_specs=pl.BlockSpec((tc, tr), lambda i, j: (j, i)),
        out_shape=jax.ShapeDtypeStruct((cols, rows), BF16),
        compiler_params=_cparams(("arbitrary", "arbitrary")),
        name=name,
    )(x)


def _ada_norm(x, g, sc, sh):
    y = x * lax.rsqrt(jnp.mean(x * x, axis=-1, keepdims=True) + RMS_EPS) * g
    return y * (1.0 + sc) + sh


def _inproj_kernel(x_ref, g_ref, sc_ref, sh_ref, w_ref, o_ref, ob_ref, h_scr):
    @pl.when(pl.program_id(1) == 0)
    def _():
        h_scr[...] = _ada_norm(x_ref[...], g_ref[...], sc_ref[...], sh_ref[...]).astype(BF16)

    z = _dot(h_scr[...], w_ref[...])
    o_ref[...] = z
    ob_ref[...] = z.astype(BF16)


def _inproj(x, g, sc, sh, w_p):
    S, D = x.shape
    N = w_p.shape[1]
    tm, tn = 1024, 512
    vec = pl.BlockSpec((1, D), lambda i, j: (0, 0))
    out = pl.BlockSpec((tm, tn), lambda i, j: (i, j))
    return pl.pallas_call(
        _inproj_kernel,
        grid=(S // tm, N // tn),
        in_specs=[pl.BlockSpec((tm, D), lambda i, j: (i, 0)), vec, vec, vec,
                  pl.BlockSpec((D, tn), lambda i, j: (0, j))],
        out_specs=[out, out],
        out_shape=[jax.ShapeDtypeStruct((S, N), F32), jax.ShapeDtypeStruct((S, N), BF16)],
        scratch_shapes=[pltpu.VMEM((tm, D), BF16)],
        compiler_params=_cparams(("arbitrary", "arbitrary")),
        name="norm_inproj",
    )(x, g, sc, sh, w_p)


def _forget_kernel(ff_ref, fb_ref, fcol_ref, frow_ref, carry):
    @pl.when(pl.program_id(0) == 0)
    def _():
        carry[...] = jnp.zeros_like(carry)

    tm = ff_ref.shape[0]
    logf = jax.nn.log_sigmoid(ff_ref[...] + fb_ref[...])
    r = lax.broadcasted_iota(I32, (tm, tm), 0)
    c = lax.broadcasted_iota(I32, (tm, tm), 1)
    tril = jnp.where(c <= r, 1.0, 0.0).astype(BF16)
    hi, mid, lo = _split3(logf)
    cum = _dot(tril, hi) + _dot(tril, mid) + _dot(tril, lo) + carry[...]
    carry[...] = cum[tm - 1:tm, :]
    fcol_ref[...] = cum
    frow_ref[...] = cum.T[0:SUBLANE, :]


def _forget_cumsum(z, fbias_pad):
    S = z.shape[0]
    tm = 256
    return pl.pallas_call(
        _forget_kernel,
        grid=(S // tm,),
        in_specs=[pl.BlockSpec((tm, LANE), lambda i: (i, COL_FF // LANE)),
                  pl.BlockSpec((1, LANE), lambda i: (0, 0))],
        out_specs=[pl.BlockSpec((tm, LANE), lambda i: (i, 0)),
                   pl.BlockSpec((SUBLANE, tm), lambda i: (0, i))],
        out_shape=[jax.ShapeDtypeStruct((S, LANE), F32), jax.ShapeDtypeStruct((SUBLANE, S), F32)],
        scratch_shapes=[pltpu.VMEM((1, LANE), F32)],
        compiler_params=_cparams(("arbitrary",)),
        name="fox_forget_cumsum",
    )(z, fbias_pad)


def _paired_tiles(p, t, nq):
    first = t <= p
    return jnp.where(first, p, nq - 1 - p), jnp.where(first, t, t - p - 1)


def _flash_kernel(*refs, fox, nh, tq, tk):
    if fox:
        q_ref, k_ref, vt_ref, fq_ref, fk_ref, o_ref, m_scr, l_scr, acc_scr, s_scr, p_scr = refs
    else:
        q_ref, k_ref, vt_ref, b_ref, o_ref, m_scr, l_scr, acc_scr, s_scr, p_scr = refs
    i, j = _paired_tiles(pl.program_id(0), pl.program_id(1), 2 * pl.num_programs(0))

    @pl.when(j == 0)
    def _():
        m_scr[...] = jnp.full_like(m_scr, NEG)
        l_scr[...] = jnp.zeros_like(l_scr)
        acc_scr[...] = jnp.zeros_like(acc_scr)

    def step(diagonal):
        for h in range(nh):
            sl = slice(h * HEAD_DIM, (h + 1) * HEAD_DIM)
            s_scr[h] = _dot_nt(k_ref[:, sl], q_ref[:, sl])
        if fox:
            if diagonal:
                keys = j * tk + lax.broadcasted_iota(I32, (tk, tq), 0)
                qrys = i * tq + lax.broadcasted_iota(I32, (tk, tq), 1)
                causal = keys <= qrys
        else:
            bias = b_ref[...].astype(F32)
        m_all = m_scr[...]
        l_all = l_scr[...]
        m_rows, l_rows, alphas = [], [], []
        for h in range(nh):
            s = s_scr[h]
            if fox:
                s = s + (fq_ref[h:h + 1, :] - fk_ref[:, h:h + 1])
                if diagonal:
                    s = jnp.where(causal, s, NEG)
            else:
                s = s + bias
            m_prev = m_all[h:h + 1, :]
            m_new = jnp.maximum(m_prev, jnp.max(s, axis=0, keepdims=True))
            alpha = jnp.exp(m_prev - m_new)
            p = jnp.exp(s - m_new)
            l_rows.append(alpha * l_all[h:h + 1, :] + jnp.sum(p, axis=0, keepdims=True))
            m_rows.append(m_new)
            alphas.append(alpha)
            p_scr[h] = p.astype(BF16)
        m_scr[...] = jnp.concatenate(m_rows + [m_all[nh:, :]], axis=0)
        l_scr[...] = jnp.concatenate(l_rows + [l_all[nh:, :]], axis=0)
        for h in range(nh):
            sl = slice(h * HEAD_DIM, (h + 1) * HEAD_DIM)
            acc_scr[sl, :] = alphas[h] * acc_scr[sl, :] + _dot(vt_ref[sl, :], p_scr[h])

    r = tq // tk
    j_last = r * i + (r - 1)
    if fox:
        pl.when(j < r * i)(lambda: step(False))
        pl.when(jnp.logical_and(j >= r * i, j <= j_last))(lambda: step(True))
    else:
        pl.when(j <= j_last)(lambda: step(False))

    @pl.when(j == j_last)
    def _():
        for h in range(nh):
            sl = slice(h * HEAD_DIM, (h + 1) * HEAD_DIM)
            o_ref[:, sl] = (acc_scr[sl, :] / l_scr[h:h + 1, :]).T


def _flash(zb, vt, col_q, col_k, nh, *, fcol=None, frow=None, bias_t=None):
    S = zb.shape[0]
    W = nh * HEAD_DIM
    tq = tk = 512
    nq = S // tq
    assert nq % 2 == 0
    fox = bias_t is None
    qi = lambda p, t: _paired_tiles(p, t, nq)[0]
    kj = lambda p, t: _paired_tiles(p, t, nq)[1]
    qspec = pl.BlockSpec((tq, W), lambda p, t: (qi(p, t), col_q // W))
    kspec = pl.BlockSpec((tk, W), lambda p, t: (kj(p, t), col_k // W))
    vspec = pl.BlockSpec((W, tk), lambda p, t: (0, kj(p, t)))
    if fox:
        extra = [frow, fcol]
        especs = [pl.BlockSpec((SUBLANE, tq), lambda p, t: (0, qi(p, t))),
                  pl.BlockSpec((tk, LANE), lambda p, t: (kj(p, t), 0))]
    else:
        extra = [bias_t]
        especs = [pl.BlockSpec((tk, tq), lambda p, t: (kj(p, t), qi(p, t)))]
    return pl.pallas_call(
        functools.partial(_flash_kernel, fox=fox, nh=nh, tq=tq, tk=tk),
        grid=(nq // 2, nq + 1),
        in_specs=[qspec, kspec, vspec] + especs,
        out_specs=pl.BlockSpec((tq, W), lambda p, t: (qi(p, t), 0)),
        out_shape=jax.ShapeDtypeStruct((S, W), F32),
        scratch_shapes=[pltpu.VMEM((SUBLANE, tq), F32), pltpu.VMEM((SUBLANE, tq), F32),
                        pltpu.VMEM((W, tq), F32), pltpu.VMEM((nh, tk, tq), F32),
                        pltpu.VMEM((nh, tk, tq), BF16)],
        compiler_params=_cparams(("arbitrary", "arbitrary")),
        name="fox_flash" if fox else "dsa_flash",
    )(zb, zb, vt, *extra)


def _fox_attention(z, zb, fcol, frow):
    vt = _transpose_cast(z, None, z.shape[0], FOX_W, COL_FV, 256, 256, "fox_v_transpose")
    return _flash(zb, vt, COL_FQ, COL_FK, FOX_HEADS, fcol=fcol, frow=frow)


def _dsa_attention(z, zb, bias_t):
    vt = _transpose_cast(z, None, z.shape[0], DSA_W, COL_DV, 256, 256, "dsa_v_transpose")
    return _flash(zb, vt, COL_DQ, COL_DK, DSA_HEADS, bias_t=bias_t)


def _gla_kernel(q_ref, k_ref, v_ref, r_ref, a_ref, wa_ref, ba_ref, gn_ref, o_ref, st_scr):
    @pl.when(pl.program_id(0) == 0)
    def _():
        st_scr[...] = jnp.zeros_like(st_scr)

    C = GLA_CHUNK
    tm = q_ref.shape[0]
    r = lax.broadcasted_iota(I32, (C, C), 0)
    c = lax.broadcasted_iota(I32, (C, C), 1)
    lower = c <= r
    tril = jnp.where(lower, 1.0, 0.0).astype(BF16)

    a_hi, a_mid, _ = _split3(a_ref[...])
    w_hi, w_mid, _ = _split3(wa_ref[...])
    gate_x = _dot(a_hi, w_hi) + (_dot(a_hi, w_mid) + _dot(a_mid, w_hi)) + ba_ref[...]
    log_a = jax.nn.log_sigmoid(gate_x) / GLA_GATE_TAU

    for hd in range(GLA_HEADS):
        cols = slice(hd * HEAD_DIM, (hd + 1) * HEAD_DIM)
        st = st_scr[hd]
        for ci in range(tm // C):
            rows = slice(ci * C, (ci + 1) * C)
            la_hi, la_mid, la_lo = _split3(log_a[rows, cols])
            g = _dot(tril, la_hi) + _dot(tril, la_mid) + _dot(tril, la_lo)
            q = q_ref[rows, cols]
            k = k_ref[rows, cols]
            v = v_ref[rows, cols].astype(BF16)
            g_mid = g[C // 2:C // 2 + 1, :]
            g_last = g[C - 1:C, :]
            inter = _dot_nt((q * jnp.exp(g)).astype(BF16), st.astype(BF16))
            qr = (q * jnp.exp(jnp.minimum(g - g_mid, 80.0))).astype(BF16)
            kr = (k * jnp.exp(jnp.minimum(g_mid - g, 80.0))).astype(BF16)
            att = jnp.where(lower, _dot_nt(qr, kr), 0.0)
            intra = _dot(att.astype(BF16), v)
            kd = (k * jnp.exp(g_last - g)).astype(BF16)
            st = st * jnp.exp(g_last) + _dot_tn(v, kd)
            go = inter + intra
            y = go * lax.rsqrt(jnp.mean(go * go, axis=-1, keepdims=True) + RMS_EPS) * gn_ref[...]
            o_ref[rows, cols] = y * jax.nn.silu(r_ref[rows, cols])
        st_scr[hd] = st


def _gla(z, wa2_pad, ba, gnorm):
    S = z.shape[0]
    tm = 256
    grp = lambda col: pl.BlockSpec((tm, GLA_W), lambda i: (i, col // GLA_W))
    return pl.pallas_call(
        _gla_kernel,
        grid=(S // tm,),
        in_specs=[grp(COL_GQ), grp(COL_GK), grp(COL_GV), grp(COL_GR),
                  pl.BlockSpec((tm, LANE), lambda i: (i, COL_GA // LANE)),
                  pl.BlockSpec((LANE, GLA_W), lambda i: (0, 0)),
                  pl.BlockSpec((1, GLA_W), lambda i: (0, 0)),
                  pl.BlockSpec((1, LANE), lambda i: (0, 0))],
        out_specs=pl.BlockSpec((tm, GLA_W), lambda i: (i, 0)),
        out_shape=jax.ShapeDtypeStruct((S, GLA_W), F32),
        scratch_shapes=[pltpu.VMEM((GLA_HEADS, HEAD_DIM, HEAD_DIM), F32)],
        compiler_params=_cparams(("arbitrary",)),
        name="gla_chunked",
    )(z, z, z, z, z, wa2_pad, ba, gnorm)


def _sort_key(x):
    u = pltpu.bitcast(x, I32)
    return u ^ ((u >> 31) & 0x7FFFFFFF)


def _indexer_kernel(iq_ref, ik_ref, iw_ref, o_ref, key_scr, *, tq, topk):
    i = pl.program_id(0)
    S = ik_ref.shape[0]
    nk = S // tq
    wt = iw_ref[...].T[0:IDX_HEADS, :] * (IDX_HEADS ** -0.5 * IDX_DIM ** -0.5)
    qrys = i * tq + lax.broadcasted_iota(I32, (tq, tq), 1)
    key0 = lax.broadcasted_iota(I32, (tq, tq), 0)

    def score_chunk(jc, carry):
        off = pl.multiple_of(jc * tq, tq)
        ikb = ik_ref[pl.ds(off, tq), :]
        acc = jnp.zeros((tq, tq), F32)
        for h in range(IDX_HEADS):
            qh = iq_ref[:, h * LANE:(h + 1) * LANE]
            acc = acc + jnp.maximum(_dot_nt(ikb, qh), 0.0) * wt[h:h + 1, :]
        key_scr[pl.ds(off, tq), :] = jnp.where(off + key0 <= qrys, _sort_key(acc), INT_MIN)
        return carry

    lax.fori_loop(0, i + 1, score_chunk, 0)

    def bisect(it, thr):
        cand = thr + lax.shift_left(jnp.int32(1), 31 - it)

        def count_chunk(jc, cnt):
            off = pl.multiple_of(jc * tq, tq)
            ge = jnp.where(key_scr[pl.ds(off, tq), :] >= cand, 1, 0)
            return cnt + jnp.sum(ge.reshape(tq // SUBLANE, SUBLANE, tq), axis=0)

        cnt = lax.fori_loop(0, i + 1, count_chunk, jnp.zeros((SUBLANE, tq), I32))
        return jnp.where(jnp.sum(cnt, axis=0, keepdims=True) >= topk, cand, thr)

    thr = lax.fori_loop(0, 32, bisect, jnp.full((1, tq), INT_MIN, I32))
    thr = jnp.maximum(thr, INT_MIN + 1)

    def write_chunk(jc, carry):
        off = pl.multiple_of(jc * tq, tq)
        keep = key_scr[pl.ds(off, tq), :] >= thr
        o_ref[pl.ds(off, tq), :] = jnp.where(keep, 0.0, NEG).astype(BF16)
        return carry

    lax.fori_loop(0, i + 1, write_chunk, 0)

    def fill_chunk(jc, carry):
        off = pl.multiple_of(jc * tq, tq)
        o_ref[pl.ds(off, tq), :] = jnp.full((tq, tq), NEG, BF16)
        return carry

    lax.fori_loop(i + 1, nk, fill_chunk, 0)


def _dsa_mask(z, zb):
    S = z.shape[0]
    tq = 256
    topk = min(DSA_MAX_TOPK, S // 4)
    return pl.pallas_call(
        functools.partial(_indexer_kernel, tq=tq, topk=topk),
        grid=(S // tq,),
        in_specs=[pl.BlockSpec((tq, IDX_HEADS * LANE), lambda i: (i, COL_IQ // (IDX_HEADS * LANE))),
                  pl.BlockSpec((S, LANE), lambda i: (0, COL_IK // LANE)),
                  pl.BlockSpec((tq, LANE), lambda i: (i, COL_IW // LANE))],
        out_specs=pl.BlockSpec((S, tq), lambda i: (0, i)),
        out_shape=jax.ShapeDtypeStruct((S, S), BF16),
        scratch_shapes=[pltpu.VMEM((S, tq), I32)],
        compiler_params=_cparams(("arbitrary",)),
        name="dsa_indexer_mask",
    )(zb, zb, z)


def _outproj_kernel(fox_ref, gla_ref, dsa_ref, w_ref, x_ref, g1_ref, n2_ref, sc_ref, sh_ref,
                    xo_ref, h_ref):
    acc = _dot(fox_ref[...].astype(BF16), w_ref[0:FOX_W, :])
    acc = acc + _dot(gla_ref[...].astype(BF16), w_ref[FOX_W:FOX_W + GLA_W, :])
    acc = acc + _dot(dsa_ref[...].astype(BF16), w_ref[FOX_W + GLA_W:, :])
    xn = x_ref[...] + g1_ref[...] * acc
    xo_ref[...] = xn
    h_ref[...] = _ada_norm(xn, n2_ref[...], sc_ref[...], sh_ref[...]).astype(BF16)


def _outproj(fox, gla, dsa, w_out, x, g1, n2, sc2, sh2):
    S, D = x.shape
    tm = 256
    vec = pl.BlockSpec((1, D), lambda i: (0, 0))
    row = lambda w: pl.BlockSpec((tm, w), lambda i: (i, 0))
    return pl.pallas_call(
        _outproj_kernel,
        grid=(S // tm,),
        in_specs=[row(FOX_W), row(GLA_W), row(DSA_W), pl.BlockSpec((D, D), lambda i: (0, 0)),
                  row(D), vec, vec, vec, vec],
        out_specs=[row(D), row(D)],
        out_shape=[jax.ShapeDtypeStruct((S, D), F32), jax.ShapeDtypeStruct((S, D), BF16)],
        compiler_params=_cparams(("arbitrary",)),
        name="outproj_residual_norm",
    )(fox, gla, dsa, w_out, x, g1, n2, sc2, sh2)


def _top_rows(x, n):
    rows = x.shape[0]
    ridx = lax.broadcasted_iota(I32, x.shape, 0)
    rank = jnp.full(x.shape, float(n), F32)
    vals = []
    for k in range(n):
        m = jnp.max(x, axis=0, keepdims=True)
        first = jnp.min(jnp.where(x == m, ridx, rows), axis=0, keepdims=True)
        sel = ridx == first
        x = jnp.where(sel, -jnp.inf, x)
        rank = jnp.where(sel, float(k), rank)
        vals.append(m)
    return jnp.concatenate(vals, axis=0), rank


def _peer_score_kernel(h_ref, wq_ref, k1_ref, k2_ref, n_ref, e1_ref, r2_ref, e2_ref, q_scr):
    q = _dot(h_ref[...], wq_ref[...])
    for h in range(PEER_HEADS):
        q_scr[h] = q[:, h * PEER_DQ:(h + 1) * PEER_DQ].astype(BF16)
    half = PEER_DQ // 2
    K = PEER_TOPK

    for h in range(PEER_HEADS):
        qh = q_scr[h]
        s1 = _dot_nt(k1_ref[h], qh[:, :half])
        s2 = _dot_nt(k2_ref[h], qh[:, half:])
        v1, r1 = _top_rows(s1, K)
        v2, r2 = _top_rows(s2, K)
        groups = ([v1[0:1, :] + v2] + [v1[a:a + 1, :] + v2[0:SUBLANE, :] for a in range(1, SUBLANE)]
                  + [v1[SUBLANE:K, :] + v2[0:1, :]])
        best, _ = _top_rows(jnp.concatenate(groups, axis=0), K)
        tau = best[K - 1:K, :]
        zsum = jnp.sum(jnp.exp(best - best[0:1, :]), axis=0, keepdims=True)
        n = jnp.zeros(s1.shape, F32)
        for a in range(SUBLANE):
            n_a = jnp.sum(jnp.where(groups[a] >= tau, 1.0, 0.0), axis=0, keepdims=True)
            n = jnp.where(r1 == float(a), n_a, n)
        tail = jnp.where(groups[SUBLANE] >= tau, 1.0, 0.0)
        for a in range(SUBLANE, K):
            n = jnp.where(r1 == float(a), tail[a - SUBLANE:a - SUBLANE + 1, :], n)
        n_ref[h] = n
        e1_ref[h] = jnp.where(r1 < float(K), jnp.exp(s1 - v1[0:1, :]), 0.0)
        r2_ref[h] = r2.astype(BF16)
        e2_ref[h] = (jnp.where(r2 < float(K), jnp.exp(s2 - v2[0:1, :]), 0.0) / zsum).astype(BF16)


def _peer_scores(h2, wq, k1, k2):
    S, D = h2.shape
    tm = 256
    big = pl.BlockSpec((PEER_HEADS, PEER_NKEYS, tm), lambda i: (0, 0, i))
    kspec = pl.BlockSpec((PEER_HEADS, PEER_NKEYS, PEER_DQ // 2), lambda i: (0, 0, 0))
    f32s = jax.ShapeDtypeStruct((PEER_HEADS, PEER_NKEYS, S), F32)
    b16s = jax.ShapeDtypeStruct((PEER_HEADS, PEER_NKEYS, S), BF16)
    return pl.pallas_call(
        _peer_score_kernel,
        grid=(S // tm,),
        in_specs=[pl.BlockSpec((tm, D), lambda i: (i, 0)),
                  pl.BlockSpec((D, PEER_HEADS * PEER_DQ), lambda i: (0, 0)), kspec, kspec],
        out_specs=[big, big, big, big],
        out_shape=[f32s, f32s, b16s, b16s],
        scratch_shapes=[pltpu.VMEM((PEER_HEADS, tm, PEER_DQ), BF16)],
        compiler_params=_cparams(("arbitrary",)),
        name="peer_scores",
    )(h2, wq, k1, k2)


def _peer_mix_kernel(h_ref, u_ref, vta_ref, vtb_ref, n_ref, e1_ref, r2_ref, e2_ref, x_ref, g2_ref,
                     o_ref, acc_scr, p_scr, w_scr, *, eb, nb):
    s = pl.program_id(1)
    last = pl.num_programs(1) - 1
    ng = eb // PEER_NKEYS
    zero = jnp.zeros((), BF16)

    @pl.when(s == 0)
    def _():
        acc_scr[...] = jnp.zeros_like(acc_scr)
        p_scr[1] = jnp.zeros(p_scr.shape[1:], F32)

    def produce(slot):
        p_scr[slot] = _dot_nt(u_ref[slot * eb:(slot + 1) * eb, :], h_ref[...])

    def consume(slot, blk, vt_ref):
        for cc in range(ng):
            rows = slice(cc * PEER_NKEYS, (cc + 1) * PEER_NKEYS)
            c = blk * ng + cc
            g = None
            for h in range(PEER_HEADS):
                keep = r2_ref[h] < n_ref[h, pl.ds(c, 1), :].astype(BF16)
                gh = jnp.where(keep, e2_ref[h], zero) * e1_ref[h, pl.ds(c, 1), :].astype(BF16)
                g = gh if g is None else g + gh
            w_scr[slot, rows, :] = g * jax.nn.gelu(p_scr[slot, rows, :]).astype(BF16)
        acc_scr[...] += _dot(vt_ref[...], w_scr[slot])

    produce(0)
    consume(1, jnp.maximum(2 * s - 1, 0), vta_ref)

    @pl.when(s < last)
    def _():
        produce(1)
        consume(0, jnp.minimum(2 * s, nb - 1), vtb_ref)

    @pl.when(s == last)
    def _():
        o_ref[...] = x_ref[...] + g2_ref[...] * acc_scr[...].T


def _peer_mix(h2, u, vt, n, e1, r2, e2, x, g2):
    S, D = x.shape
    tm, eb = 512, 512
    nb = PEER_EXPERTS // eb
    once = dict(pipeline_mode=pl.Buffered(1))
    big = pl.BlockSpec((PEER_HEADS, PEER_NKEYS, tm), lambda i, s: (0, 0, i), **once)
    row = pl.BlockSpec((tm, D), lambda i, s: (i, 0), **once)
    return pl.pallas_call(
        functools.partial(_peer_mix_kernel, eb=eb, nb=nb),
        grid=(S // tm, nb // 2 + 1),
        in_specs=[row,
                  pl.BlockSpec((2 * eb, D), lambda i, s: (jnp.minimum(s, nb // 2 - 1), 0)),
                  pl.BlockSpec((None, D, eb), lambda i, s: (jnp.maximum(2 * s - 1, 0), 0, 0)),
                  pl.BlockSpec((None, D, eb), lambda i, s: (jnp.minimum(2 * s, nb - 1), 0, 0)),
                  big, big, big, big, row, pl.BlockSpec((1, D), lambda i, s: (0, 0))],
        out_specs=pl.BlockSpec((tm, D), lambda i, s: (i, 0)),
        out_shape=jax.ShapeDtypeStruct((S, D), F32),
        scratch_shapes=[pltpu.VMEM((D, tm), F32), pltpu.VMEM((2, eb, tm), F32),
                        pltpu.VMEM((2, eb, tm), BF16)],
        compiler_params=_cparams(("arbitrary", "arbitrary")),
        name="peer_dense_mix",
    )(h2, u, vt, vt, n, e1, r2, e2, x, g2)


def _final_norm_kernel(x_ref, g_ref, o_ref):
    x = x_ref[...]
    o_ref[...] = x * lax.rsqrt(jnp.mean(x * x, axis=-1, keepdims=True) + RMS_EPS) * g_ref[...]


def _final_norm(x, g):
    S, D = x.shape
    tm = 512
    return pl.pallas_call(
        _final_norm_kernel,
        grid=(S // tm,),
        in_specs=[pl.BlockSpec((tm, D), lambda i: (i, 0)), pl.BlockSpec((1, D), lambda i: (0, 0))],
        out_specs=pl.BlockSpec((tm, D), lambda i: (i, 0)),
        out_shape=jax.ShapeDtypeStruct((S, D), F32),
        compiler_params=_cparams(("arbitrary",)),
        name="final_rmsnorm",
    )(x, g)


def kernel(x, c, ada_w, ada_b, norm1_g, norm2_g, final_g, w_in, fox_fbias, gla_wa2, gla_ba,
           gla_norm_g, w_out, peer_wq, peer_k1, peer_k2, peer_u, peer_v):
    B, S, D = x.shape
    assert B == 1 and D == D_MODEL
    L = ada_w.shape[0]
    xs = x.reshape(S, D)
    mod = _modulation(c, ada_w, ada_b)
    for l in range(L):
        sh1, sc1, g1, sh2, sc2, g2 = [mod[l, :, m * D:(m + 1) * D] for m in range(N_MOD)]
        z, zb = _inproj(xs, norm1_g[l].reshape(1, D), sc1, sh1, _layout_w_in(w_in, l))

        fcol, frow = _forget_cumsum(z, jnp.pad(fox_fbias[l], (0, LANE - FOX_HEADS)).reshape(1, LANE))
        fox = _fox_attention(z, zb, fcol, frow)

        wa2_pad = jnp.pad(gla_wa2[l], ((0, LANE - GLA_GATE_RANK), (0, 0)))
        gla = _gla(z, wa2_pad, gla_ba[l].reshape(1, GLA_W), gla_norm_g[l].reshape(1, HEAD_DIM))

        dsa = _dsa_attention(z, zb, _dsa_mask(z, zb))

        xs, h2 = _outproj(fox, gla, dsa, _cast_layer(w_out, l, "w_out_cast"), xs, g1,
                          norm2_g[l].reshape(1, D), sc2, sh2)

        n, e1, r2, e2 = _peer_scores(h2, _cast_layer(peer_wq, l, "peer_wq_cast"),
                                     peer_k1[l].astype(BF16), peer_k2[l].astype(BF16))
        vt = _transpose_cast(peer_v, l, PEER_EXPERTS, D, 0, 512, 512, "peer_v_transpose",
                             blocked=True)
        xs = _peer_mix(h2, _cast_layer(peer_u, l, "peer_u_cast"), vt, n, e1, r2, e2, xs, g2)
    return _final_norm(xs, final_g.reshape(1, D)).reshape(B, S, D)
```

```python
import functools

import jax
import jax.numpy as jnp
from jax import lax
from jax.experimental import pallas as pl
from jax.experimental.pallas import tpu as pltpu

F32 = jnp.float32
BF16 = jnp.bfloat16
I32 = jnp.int32

D_MODEL = 2048
HEAD_DIM = 128
FOX_HEADS = 6
GLA_HEADS = 4
DSA_HEADS = 6
FOX_W = FOX_HEADS * HEAD_DIM
GLA_W = GLA_HEADS * HEAD_DIM
DSA_W = DSA_HEADS * HEAD_DIM
GLA_GATE_RANK = 16
GLA_GATE_TAU = 16.0
GLA_CHUNK = 64
IDX_HEADS = 16
IDX_DIM = 64
DSA_MAX_TOPK = 256
PEER_HEADS = 8
PEER_DQ = 256
PEER_NKEYS = 128
PEER_EXPERTS = PEER_NKEYS * PEER_NKEYS
PEER_TOPK = 16
RMS_EPS = 1e-6
N_MOD = 6
N_IN = 3 * FOX_W + FOX_HEADS + 4 * GLA_W + GLA_GATE_RANK + 3 * DSA_W + IDX_HEADS * IDX_DIM + IDX_DIM + IDX_HEADS

LANE = 128
SUBLANE = 8
NEG = -1e30
INT_MIN = -(2 ** 31)
VMEM_LIMIT = 56 * 1024 * 1024
ATTN_SCALE = HEAD_DIM ** -0.5

COL_IQ = 0
COL_GQ = 2048
COL_GK = 2560
COL_GV = 3072
COL_GR = 3584
COL_FF = 4096
COL_GA = 4224
COL_IK = 4352
COL_IW = 4480
COL_FQ = 4608
COL_FK = 5376
COL_FV = 6144
COL_DQ = 6912
COL_DK = 7680
COL_DV = 8448
N_IN_PAD = 9216


def _w_in_groups():
    sizes = (("fq", FOX_W), ("fk", FOX_W), ("fv", FOX_W), ("ff", FOX_HEADS), ("gq", GLA_W),
             ("gk", GLA_W), ("gv", GLA_W), ("gr", GLA_W), ("ga", GLA_GATE_RANK), ("dq", DSA_W),
             ("dk", DSA_W), ("dv", DSA_W), ("iq", IDX_HEADS * IDX_DIM), ("ik", IDX_DIM),
             ("iw", IDX_HEADS))
    dst = dict(fq=COL_FQ, fk=COL_FK, fv=COL_FV, ff=COL_FF, gq=COL_GQ, gk=COL_GK, gv=COL_GV,
               gr=COL_GR, ga=COL_GA, dq=COL_DQ, dk=COL_DK, dv=COL_DV, ik=COL_IK, iw=COL_IW)
    scale = dict(fq=ATTN_SCALE, dq=ATTN_SCALE, gq=ATTN_SCALE)
    groups, o = [], 0
    for name, n in sizes:
        if name == "iq":
            for h in range(IDX_HEADS):
                groups.append((o + h * IDX_DIM, IDX_DIM, COL_IQ + h * LANE, 1.0))
        else:
            groups.append((o, n, dst[name], scale.get(name, 1.0)))
        o += n
    assert o == N_IN
    return tuple(groups)


def _cparams(sem):
    return pltpu.CompilerParams(dimension_semantics=sem, vmem_limit_bytes=VMEM_LIMIT)


def _split3(x):
    hi = x.astype(BF16)
    r1 = x - hi.astype(F32)
    mid = r1.astype(BF16)
    lo = (r1 - mid.astype(F32)).astype(BF16)
    return hi, mid, lo


def _dot(a, b):
    return jnp.dot(a, b, preferred_element_type=F32)


def _dot_nt(a, b):
    return lax.dot_general(a, b, (((1,), (1,)), ((), ())), preferred_element_type=F32)


def _dot_tn(a, b):
    return lax.dot_general(a, b, (((0,), (0,)), ((), ())), preferred_element_type=F32)


def _mod_kernel(c_ref, w_ref, b_ref, o_ref):
    c = c_ref[...]
    ca = jax.nn.silu(c)
    o_ref[0] = jnp.sum(w_ref[0] * ca, axis=0, keepdims=True) + b_ref[0]


def _modulation(c, ada_w, ada_b):
    L, D, E = ada_w.shape
    tn = 1024
    return pl.pallas_call(
        _mod_kernel,
        grid=(L, E // tn),
        in_specs=[pl.BlockSpec((D, 1), lambda l, j: (0, 0)),
                  pl.BlockSpec((1, D, tn), lambda l, j: (l, 0, j)),
                  pl.BlockSpec((1, 1, tn), lambda l, j: (l, 0, j))],
        out_specs=pl.BlockSpec((1, 1, tn), lambda l, j: (l, 0, j)),
        out_shape=jax.ShapeDtypeStruct((L, 1, E), F32),
        compiler_params=_cparams(("arbitrary", "arbitrary")),
        name="adaln_mod",
    )(c.reshape(D, 1), ada_w, ada_b.reshape(L, 1, E))


def _relayout_kernel(w_ref, o_ref):
    o_ref[...] = jnp.zeros_like(o_ref)
    for src, n, dst, scale in _w_in_groups():
        w = w_ref[:, src:src + n]
        if scale != 1.0:
            w = w * scale
        o_ref[:, dst:dst + n] = w.astype(BF16)


def _layout_w_in(w_in, l):
    _, D, N = w_in.shape
    tr = 256
    return pl.pallas_call(
        _relayout_kernel,
        grid=(D // tr,),
        in_specs=[pl.BlockSpec((None, tr, N), lambda i: (l, i, 0))],
        out_specs=pl.BlockSpec((tr, N_IN_PAD), lambda i: (i, 0)),
        out_shape=jax.ShapeDtypeStruct((D, N_IN_PAD), BF16),
        compiler_params=_cparams(("arbitrary",)),
        name="w_in_relayout",
    )(w_in)


def _cast_kernel(x_ref, o_ref):
    o_ref[...] = x_ref[...].astype(BF16)


def _cast_layer(x, layer, name):
    _, R, C = x.shape
    tr = 512
    return pl.pallas_call(
        _cast_kernel,
        grid=(R // tr,),
        in_specs=[pl.BlockSpec((None, tr, C), lambda i: (layer, i, 0))],
        out_specs=pl.BlockSpec((tr, C), lambda i: (i, 0)),
        out_shape=jax.ShapeDtypeStruct((R, C), BF16),
        compiler_params=_cparams(("arbitrary",)),
        name=name,
    )(x)


def _transpose_cast_kernel(x_ref, o_ref):
    o_ref[...] = x_ref[...].T.astype(BF16)


def _transpose_cast(x, layer, rows, cols, col0, tr, tc, name, blocked=False):
    if layer is None:
        spec = pl.BlockSpec((tr, tc), lambda i, j: (i, col0 // tc + j))
    else:
        spec = pl.BlockSpec((None, tr, tc), lambda i, j: (layer, i, col0 // tc + j))
    if blocked:
        return pl.pallas_call(
            _transpose_cast_kernel,
            grid=(rows // tr, cols // tc),
            in_specs=[spec],
            out_specs=pl.BlockSpec((None, tc, tr), lambda i, j: (i, j, 0)),
            out_shape=jax.ShapeDtypeStruct((rows // tr, cols, tr), BF16),
            compiler_params=_cparams(("arbitrary", "arbitrary")),
            name=name,
        )(x)
    return pl.pallas_call(
        _transpose_cast_kernel,
        grid=(rows // tr, cols // tc),
        in_specs=[spec],
        out_specs=pl.BlockSpec((tc, tr), lambda i, j: (j, i)),
        out_shape=jax.ShapeDtypeStruct((cols, rows), BF16),
        compiler_params=_cparams(("arbitrary", "arbitrary")),
        name=name,
    )(x)


def _ada_norm(x, g, sc, sh):
    y = x * lax.rsqrt(jnp.mean(x * x, axis=-1, keepdims=True) + RMS_EPS) * g
    return y * (1.0 + sc) + sh


def _inproj_kernel(x_ref, g_ref, sc_ref, sh_ref, w_ref, o_ref, ob_ref, h_scr):
    @pl.when(pl.program_id(1) == 0)
    def _():
        h_scr[...] = _ada_norm(x_ref[...], g_ref[...], sc_ref[...], sh_ref[...]).astype(BF16)

    z = _dot(h_scr[...], w_ref[...])
    o_ref[...] = z
    ob_ref[...] = z.astype(BF16)


def _inproj(x, g, sc, sh, w_p):
    S, D = x.shape
    N = w_p.shape[1]
    tm, tn = 1024, 1024
    vec = pl.BlockSpec((1, D), lambda i, j: (0, 0))
    out = pl.BlockSpec((tm, tn), lambda i, j: (i, j))
    return pl.pallas_call(
        _inproj_kernel,
        grid=(S // tm, N // tn),
        in_specs=[pl.BlockSpec((tm, D), lambda i, j: (i, 0)), vec, vec, vec,
                  pl.BlockSpec((D, tn), lambda i, j: (0, j))],
        out_specs=[out, out],
        out_shape=[jax.ShapeDtypeStruct((S, N), F32), jax.ShapeDtypeStruct((S, N), BF16)],
        scratch_shapes=[pltpu.VMEM((tm, D), BF16)],
        compiler_params=_cparams(("arbitrary", "arbitrary")),
        name="norm_inproj",
    )(x, g, sc, sh, w_p)


def _forget_kernel(ff_ref, fb_ref, fcol_ref, frow_ref, carry):
    @pl.when(pl.program_id(0) == 0)
    def _():
        carry[...] = jnp.zeros_like(carry)

    tm = ff_ref.shape[0]
    logf = jax.nn.log_sigmoid(ff_ref[...] + fb_ref[...])
    r = lax.broadcasted_iota(I32, (tm, tm), 0)
    c = lax.broadcasted_iota(I32, (tm, tm), 1)
    tril = jnp.where(c <= r, 1.0, 0.0).astype(BF16)
    hi, mid, lo = _split3(logf)
    cum = _dot(tril, hi) + _dot(tril, mid) + _dot(tril, lo) + carry[...]
    carry[...] = cum[tm - 1:tm, :]
    fcol_ref[...] = cum
    frow_ref[...] = cum.T[0:SUBLANE, :]


def _forget_cumsum(z, fbias_pad):
    S = z.shape[0]
    tm = 256
    return pl.pallas_call(
        _forget_kernel,
        grid=(S // tm,),
        in_specs=[pl.BlockSpec((tm, LANE), lambda i: (i, COL_FF // LANE)),
                  pl.BlockSpec((1, LANE), lambda i: (0, 0))],
        out_specs=[pl.BlockSpec((tm, LANE), lambda i: (i, 0)),
                   pl.BlockSpec((SUBLANE, tm), lambda i: (0, i))],
        out_shape=[jax.ShapeDtypeStruct((S, LANE), F32), jax.ShapeDtypeStruct((SUBLANE, S), F32)],
        scratch_shapes=[pltpu.VMEM((1, LANE), F32)],
        compiler_params=_cparams(("arbitrary",)),
        name="fox_forget_cumsum",
    )(z, fbias_pad)


def _paired_tiles(p, t, nq):
    first = t <= p
    return jnp.where(first, p, nq - 1 - p), jnp.where(first, t, t - p - 1)


def _flash_kernel(*refs, fox, nh, tq, tk):
    if fox:
        q_ref, k_ref, vt_ref, fq_ref, fk_ref, o_ref, m_scr, l_scr, acc_scr, s_scr, p_scr = refs
    else:
        q_ref, k_ref, vt_ref, b_ref, o_ref, m_scr, l_scr, acc_scr, s_scr, p_scr = refs
    i, j = _paired_tiles(pl.program_id(0), pl.program_id(1), 2 * pl.num_programs(0))

    @pl.when(j == 0)
    def _():
        m_scr[...] = jnp.full_like(m_scr, NEG)
        l_scr[...] = jnp.zeros_like(l_scr)
        acc_scr[...] = jnp.zeros_like(acc_scr)

    def step(diagonal):
        for h in range(nh):
            sl = slice(h * HEAD_DIM, (h + 1) * HEAD_DIM)
            s_scr[h] = _dot_nt(k_ref[:, sl], q_ref[:, sl])
        if fox:
            if diagonal:
                keys = j * tk + lax.broadcasted_iota(I32, (tk, tq), 0)
                qrys = i * tq + lax.broadcasted_iota(I32, (tk, tq), 1)
                causal = keys <= qrys
        else:
            bias = b_ref[...].astype(F32)
        m_all = m_scr[...]
        l_all = l_scr[...]
        m_rows, l_rows, alphas = [], [], []
        for h in range(nh):
            s = s_scr[h]
            if fox:
                s = s + (fq_ref[h:h + 1, :] - fk_ref[:, h:h + 1])
                if diagonal:
                    s = jnp.where(causal, s, NEG)
            else:
                s = s + bias
            m_prev = m_all[h:h + 1, :]
            m_new = jnp.maximum(m_prev, jnp.max(s, axis=0, keepdims=True))
            alpha = jnp.exp(m_prev - m_new)
            p = jnp.exp(s - m_new)
            l_rows.append(alpha * l_all[h:h + 1, :] + jnp.sum(p, axis=0, keepdims=True))
            m_rows.append(m_new)
            alphas.append(alpha)
            p_scr[h] = p.astype(BF16)
        m_scr[...] = jnp.concatenate(m_rows + [m_all[nh:, :]], axis=0)
        l_scr[...] = jnp.concatenate(l_rows + [l_all[nh:, :]], axis=0)
        for h in range(nh):
            sl = slice(h * HEAD_DIM, (h + 1) * HEAD_DIM)
            acc_scr[sl, :] = alphas[h] * acc_scr[sl, :] + _dot(vt_ref[sl, :], p_scr[h])

    r = tq // tk
    j_last = r * i + (r - 1)
    if fox:
        pl.when(j < r * i)(lambda: step(False))
        pl.when(jnp.logical_and(j >= r * i, j <= j_last))(lambda: step(True))
    else:
        pl.when(j <= j_last)(lambda: step(False))

    @pl.when(j == j_last)
    def _():
        for h in range(nh):
            sl = slice(h * HEAD_DIM, (h + 1) * HEAD_DIM)
            o_ref[:, sl] = (acc_scr[sl, :] / l_scr[h:h + 1, :]).T


def _flash(zb, vt, col_q, col_k, nh, *, fcol=None, frow=None, bias_t=None):
    S = zb.shape[0]
    W = nh * HEAD_DIM
    tq = tk = 512
    nq = S // tq
    assert nq % 2 == 0
    fox = bias_t is None
    qi = lambda p, t: _paired_tiles(p, t, nq)[0]
    kj = lambda p, t: _paired_tiles(p, t, nq)[1]
    qspec = pl.BlockSpec((tq, W), lambda p, t: (qi(p, t), col_q // W))
    kspec = pl.BlockSpec((tk, W), lambda p, t: (kj(p, t), col_k // W))
    vspec = pl.BlockSpec((W, tk), lambda p, t: (0, kj(p, t)))
    if fox:
        extra = [frow, fcol]
        especs = [pl.BlockSpec((SUBLANE, tq), lambda p, t: (0, qi(p, t))),
                  pl.BlockSpec((tk, LANE), lambda p, t: (kj(p, t), 0))]
    else:
        extra = [bias_t]
        especs = [pl.BlockSpec((tk, tq), lambda p, t: (kj(p, t), qi(p, t)))]
    return pl.pallas_call(
        functools.partial(_flash_kernel, fox=fox, nh=nh, tq=tq, tk=tk),
        grid=(nq // 2, nq + 1),
        in_specs=[qspec, kspec, vspec] + especs,
        out_specs=pl.BlockSpec((tq, W), lambda p, t: (qi(p, t), 0)),
        out_shape=jax.ShapeDtypeStruct((S, W), F32),
        scratch_shapes=[pltpu.VMEM((SUBLANE, tq), F32), pltpu.VMEM((SUBLANE, tq), F32),
                        pltpu.VMEM((W, tq), F32), pltpu.VMEM((nh, tk, tq), F32),
                        pltpu.VMEM((nh, tk, tq), BF16)],
        compiler_params=_cparams(("arbitrary", "arbitrary")),
        name="fox_flash" if fox else "dsa_flash",
    )(zb, zb, vt, *extra)


def _fox_attention(z, zb, fcol, frow):
    vt = _transpose_cast(z, None, z.shape[0], FOX_W, COL_FV, 256, 256, "fox_v_transpose")
    return _flash(zb, vt, COL_FQ, COL_FK, FOX_HEADS, fcol=fcol, frow=frow)


def _dsa_attention(z, zb, bias_t):
    vt = _transpose_cast(z, None, z.shape[0], DSA_W, COL_DV, 256, 256, "dsa_v_transpose")
    return _flash(zb, vt, COL_DQ, COL_DK, DSA_HEADS, bias_t=bias_t)


def _gla_kernel(q_ref, k_ref, v_ref, r_ref, a_ref, wa_ref, ba_ref, gn_ref, o_ref, st_scr):
    @pl.when(pl.program_id(0) == 0)
    def _():
        st_scr[...] = jnp.zeros_like(st_scr)

    C = GLA_CHUNK
    tm = q_ref.shape[0]
    r = lax.broadcasted_iota(I32, (C, C), 0)
    c = lax.broadcasted_iota(I32, (C, C), 1)
    lower = c <= r
    tril = jnp.where(lower, 1.0, 0.0).astype(BF16)

    a_hi, a_mid, _ = _split3(a_ref[...])
    w_hi, w_mid, _ = _split3(wa_ref[...])
    gate_x = _dot(a_hi, w_hi) + (_dot(a_hi, w_mid) + _dot(a_mid, w_hi)) + ba_ref[...]
    log_a = jax.nn.log_sigmoid(gate_x) / GLA_GATE_TAU

    for hd in range(GLA_HEADS):
        cols = slice(hd * HEAD_DIM, (hd + 1) * HEAD_DIM)
        st = st_scr[hd]
        for ci in range(tm // C):
            rows = slice(ci * C, (ci + 1) * C)
            la_hi, la_mid, la_lo = _split3(log_a[rows, cols])
            g = _dot(tril, la_hi) + _dot(tril, la_mid) + _dot(tril, la_lo)
            q = q_ref[rows, cols]
            k = k_ref[rows, cols]
            v = v_ref[rows, cols].astype(BF16)
            g_mid = g[C // 2:C // 2 + 1, :]
            g_last = g[C - 1:C, :]
            inter = _dot_nt((q * jnp.exp(g)).astype(BF16), st.astype(BF16))
            qr = (q * jnp.exp(jnp.minimum(g - g_mid, 80.0))).astype(BF16)
            kr = (k * jnp.exp(jnp.minimum(g_mid - g, 80.0))).astype(BF16)
            att = jnp.where(lower, _dot_nt(qr, kr), 0.0)
            intra = _dot(att.astype(BF16), v)
            kd = (k * jnp.exp(g_last - g)).astype(BF16)
            st = st * jnp.exp(g_last) + _dot_tn(v, kd)
            go = inter + intra
            y = go * lax.rsqrt(jnp.mean(go * go, axis=-1, keepdims=True) + RMS_EPS) * gn_ref[...]
            o_ref[rows, cols] = y * jax.nn.silu(r_ref[rows, cols])
        st_scr[hd] = st


def _gla(z, wa2_pad, ba, gnorm):
    S = z.shape[0]
    tm = 256
    grp = lambda col: pl.BlockSpec((tm, GLA_W), lambda i: (i, col // GLA_W))
    return pl.pallas_call(
        _gla_kernel,
        grid=(S // tm,),
        in_specs=[grp(COL_GQ), grp(COL_GK), grp(COL_GV), grp(COL_GR),
                  pl.BlockSpec((tm, LANE), lambda i: (i, COL_GA // LANE)),
                  pl.BlockSpec((LANE, GLA_W), lambda i: (0, 0)),
                  pl.BlockSpec((1, GLA_W), lambda i: (0, 0)),
                  pl.BlockSpec((1, LANE), lambda i: (0, 0))],
        out_specs=pl.BlockSpec((tm, GLA_W), lambda i: (i, 0)),
        out_shape=jax.ShapeDtypeStruct((S, GLA_W), F32),
        scratch_shapes=[pltpu.VMEM((GLA_HEADS, HEAD_DIM, HEAD_DIM), F32)],
        compiler_params=_cparams(("arbitrary",)),
        name="gla_chunked",
    )(z, z, z, z, z, wa2_pad, ba, gnorm)


def _sort_key(x):
    u = pltpu.bitcast(x, I32)
    return u ^ ((u >> 31) & 0x7FFFFFFF)


def _indexer_kernel(iq_ref, ik_ref, iw_ref, o_ref, key_scr, *, tq, topk):
    i = pl.program_id(0)
    S = ik_ref.shape[0]
    nk = S // tq
    wt = iw_ref[...].T[0:IDX_HEADS, :] * (IDX_HEADS ** -0.5 * IDX_DIM ** -0.5)
    qrys = i * tq + lax.broadcasted_iota(I32, (tq, tq), 1)
    key0 = lax.broadcasted_iota(I32, (tq, tq), 0)

    def score_chunk(jc, carry):
        off = pl.multiple_of(jc * tq, tq)
        ikb = ik_ref[pl.ds(off, tq), :]
        acc = jnp.zeros((tq, tq), F32)
        for h in range(IDX_HEADS):
            qh = iq_ref[:, h * LANE:(h + 1) * LANE]
            acc = acc + jnp.maximum(_dot_nt(ikb, qh), 0.0) * wt[h:h + 1, :]
        key_scr[pl.ds(off, tq), :] = jnp.where(off + key0 <= qrys, _sort_key(acc), INT_MIN)
        return carry

    lax.fori_loop(0, i + 1, score_chunk, 0)

    def bisect(it, thr):
        cand = thr + lax.shift_left(jnp.int32(1), 31 - it)

        def count_chunk(jc, cnt):
            off = pl.multiple_of(jc * tq, tq)
            ge = jnp.where(key_scr[pl.ds(off, tq), :] >= cand, 1, 0)
            return cnt + jnp.sum(ge.reshape(tq // SUBLANE, SUBLANE, tq), axis=0)

        cnt = lax.fori_loop(0, i + 1, count_chunk, jnp.zeros((SUBLANE, tq), I32))
        return jnp.where(jnp.sum(cnt, axis=0, keepdims=True) >= topk, cand, thr)

    thr = lax.fori_loop(0, 32, bisect, jnp.full((1, tq), INT_MIN, I32))
    thr = jnp.maximum(thr, INT_MIN + 1)

    def write_chunk(jc, carry):
        off = pl.multiple_of(jc * tq, tq)
        keep = key_scr[pl.ds(off, tq), :] >= thr
        o_ref[pl.ds(off, tq), :] = jnp.where(keep, 0.0, NEG).astype(BF16)
        return carry

    lax.fori_loop(0, i + 1, write_chunk, 0)

    def fill_chunk(jc, carry):
        off = pl.multiple_of(jc * tq, tq)
        o_ref[pl.ds(off, tq), :] = jnp.full((tq, tq), NEG, BF16)
        return carry

    lax.fori_loop(i + 1, nk, fill_chunk, 0)


def _dsa_mask(z, zb):
    S = z.shape[0]
    tq = 256
    topk = min(DSA_MAX_TOPK, S // 4)
    return pl.pallas_call(
        functools.partial(_indexer_kernel, tq=tq, topk=topk),
        grid=(S // tq,),
        in_specs=[pl.BlockSpec((tq, IDX_HEADS * LANE), lambda i: (i, COL_IQ // (IDX_HEADS * LANE))),
                  pl.BlockSpec((S, LANE), lambda i: (0, COL_IK // LANE)),
                  pl.BlockSpec((tq, LANE), lambda i: (i, COL_IW // LANE))],
        out_specs=pl.BlockSpec((S, tq), lambda i: (0, i)),
        out_shape=jax.ShapeDtypeStruct((S, S), BF16),
        scratch_shapes=[pltpu.VMEM((S, tq), I32)],
        compiler_params=_cparams(("arbitrary",)),
        name="dsa_indexer_mask",
    )(zb, zb, z)


def _outproj_kernel(fox_ref, gla_ref, dsa_ref, w_ref, x_ref, g1_ref, n2_ref, sc_ref, sh_ref,
                    xo_ref, h_ref):
    acc = _dot(fox_ref[...].astype(BF16), w_ref[0:FOX_W, :])
    acc = acc + _dot(gla_ref[...].astype(BF16), w_ref[FOX_W:FOX_W + GLA_W, :])
    acc = acc + _dot(dsa_ref[...].astype(BF16), w_ref[FOX_W + GLA_W:, :])
    xn = x_ref[...] + g1_ref[...] * acc
    xo_ref[...] = xn
    h_ref[...] = _ada_norm(xn, n2_ref[...], sc_ref[...], sh_ref[...]).astype(BF16)


def _outproj(fox, gla, dsa, w_out, x, g1, n2, sc2, sh2):
    S, D = x.shape
    tm = 256
    vec = pl.BlockSpec((1, D), lambda i: (0, 0))
    row = lambda w: pl.BlockSpec((tm, w), lambda i: (i, 0))
    return pl.pallas_call(
        _outproj_kernel,
        grid=(S // tm,),
        in_specs=[row(FOX_W), row(GLA_W), row(DSA_W), pl.BlockSpec((D, D), lambda i: (0, 0)),
                  row(D), vec, vec, vec, vec],
        out_specs=[row(D), row(D)],
        out_shape=[jax.ShapeDtypeStruct((S, D), F32), jax.ShapeDtypeStruct((S, D), BF16)],
        compiler_params=_cparams(("arbitrary",)),
        name="outproj_residual_norm",
    )(fox, gla, dsa, w_out, x, g1, n2, sc2, sh2)


def _top_rows(x, n):
    rows = x.shape[0]
    ridx = lax.broadcasted_iota(I32, x.shape, 0)
    rank = jnp.full(x.shape, float(n), F32)
    vals = []
    for k in range(n):
        m = jnp.max(x, axis=0, keepdims=True)
        first = jnp.min(jnp.where(x == m, ridx, rows), axis=0, keepdims=True)
        sel = ridx == first
        x = jnp.where(sel, -jnp.inf, x)
        rank = jnp.where(sel, float(k), rank)
        vals.append(m)
    return jnp.concatenate(vals, axis=0), rank


def _peer_score_kernel(h_ref, wq_ref, k1_ref, k2_ref, n_ref, e1_ref, r2_ref, e2_ref, q_scr):
    q = _dot(h_ref[...], wq_ref[...])
    for h in range(PEER_HEADS):
        q_scr[h] = q[:, h * PEER_DQ:(h + 1) * PEER_DQ].astype(BF16)
    half = PEER_DQ // 2
    K = PEER_TOPK

    for h in range(PEER_HEADS):
        qh = q_scr[h]
        s1 = _dot_nt(k1_ref[h], qh[:, :half])
        s2 = _dot_nt(k2_ref[h], qh[:, half:])
        v1, r1 = _top_rows(s1, K)
        v2, r2 = _top_rows(s2, K)
        groups = ([v1[0:1, :] + v2] + [v1[a:a + 1, :] + v2[0:SUBLANE, :] for a in range(1, SUBLANE)]
                  + [v1[SUBLANE:K, :] + v2[0:1, :]])
        best, _ = _top_rows(jnp.concatenate(groups, axis=0), K)
        tau = best[K - 1:K, :]
        zsum = jnp.sum(jnp.exp(best - best[0:1, :]), axis=0, keepdims=True)
        n = jnp.zeros(s1.shape, F32)
        for a in range(SUBLANE):
            n_a = jnp.sum(jnp.where(groups[a] >= tau, 1.0, 0.0), axis=0, keepdims=True)
            n = jnp.where(r1 == float(a), n_a, n)
        tail = jnp.where(groups[SUBLANE] >= tau, 1.0, 0.0)
        for a in range(SUBLANE, K):
            n = jnp.where(r1 == float(a), tail[a - SUBLANE:a - SUBLANE + 1, :], n)
        n_ref[h] = n
        e1_ref[h] = jnp.where(r1 < float(K), jnp.exp(s1 - v1[0:1, :]), 0.0)
        r2_ref[h] = r2.astype(BF16)
        e2_ref[h] = (jnp.where(r2 < float(K), jnp.exp(s2 - v2[0:1, :]), 0.0) / zsum).astype(BF16)


def _peer_scores(h2, wq, k1, k2):
    S, D = h2.shape
    tm = 256
    big = pl.BlockSpec((PEER_HEADS, PEER_NKEYS, tm), lambda i: (0, 0, i))
    kspec = pl.BlockSpec((PEER_HEADS, PEER_NKEYS, PEER_DQ // 2), lambda i: (0, 0, 0))
    f32s = jax.ShapeDtypeStruct((PEER_HEADS, PEER_NKEYS, S), F32)
    b16s = jax.ShapeDtypeStruct((PEER_HEADS, PEER_NKEYS, S), BF16)
    return pl.pallas_call(
        _peer_score_kernel,
        grid=(S // tm,),
        in_specs=[pl.BlockSpec((tm, D), lambda i: (i, 0)),
                  pl.BlockSpec((D, PEER_HEADS * PEER_DQ), lambda i: (0, 0)), kspec, kspec],
        out_specs=[big, big, big, big],
        out_shape=[f32s, f32s, b16s, b16s],
        scratch_shapes=[pltpu.VMEM((PEER_HEADS, tm, PEER_DQ), BF16)],
        compiler_params=_cparams(("arbitrary",)),
        name="peer_scores",
    )(h2, wq, k1, k2)


def _peer_mix_kernel(h_ref, u_ref, vta_ref, vtb_ref, n_ref, e1_ref, r2_ref, e2_ref, x_ref, g2_ref,
                     o_ref, acc_scr, p_scr, w_scr, *, eb, nb):
    s = pl.program_id(1)
    last = pl.num_programs(1) - 1
    ng = eb // PEER_NKEYS
    zero = jnp.zeros((), BF16)

    @pl.when(s == 0)
    def _():
        acc_scr[...] = jnp.zeros_like(acc_scr)
        p_scr[1] = jnp.zeros(p_scr.shape[1:], F32)

    def produce(slot):
        p_scr[slot] = _dot_nt(u_ref[slot * eb:(slot + 1) * eb, :], h_ref[...])

    def consume(slot, blk, vt_ref):
        for cc in range(ng):
            rows = slice(cc * PEER_NKEYS, (cc + 1) * PEER_NKEYS)
            c = blk * ng + cc
            g = None
            for h in range(PEER_HEADS):
                keep = r2_ref[h] < n_ref[h, pl.ds(c, 1), :].astype(BF16)
                gh = jnp.where(keep, e2_ref[h], zero) * e1_ref[h, pl.ds(c, 1), :].astype(BF16)
                g = gh if g is None else g + gh
            w_scr[slot, rows, :] = g * jax.nn.gelu(p_scr[slot, rows, :]).astype(BF16)
        acc_scr[...] += _dot(vt_ref[...], w_scr[slot])

    produce(0)
    consume(1, jnp.maximum(2 * s - 1, 0), vta_ref)

    @pl.when(s < last)
    def _():
        produce(1)
        consume(0, jnp.minimum(2 * s, nb - 1), vtb_ref)

    @pl.when(s == last)
    def _():
        o_ref[...] = x_ref[...] + g2_ref[...] * acc_scr[...].T


def _peer_mix(h2, u, vt, n, e1, r2, e2, x, g2):
    S, D = x.shape
    tm, eb = 512, 512
    nb = PEER_EXPERTS // eb
    once = dict(pipeline_mode=pl.Buffered(1))
    big = pl.BlockSpec((PEER_HEADS, PEER_NKEYS, tm), lambda i, s: (0, 0, i), **once)
    row = pl.BlockSpec((tm, D), lambda i, s: (i, 0), **once)
    return pl.pallas_call(
        functools.partial(_peer_mix_kernel, eb=eb, nb=nb),
        grid=(S // tm, nb // 2 + 1),
        in_specs=[row,
                  pl.BlockSpec((2 * eb, D), lambda i, s: (jnp.minimum(s, nb // 2 - 1), 0)),
                  pl.BlockSpec((None, D, eb), lambda i, s: (jnp.maximum(2 * s - 1, 0), 0, 0)),
                  pl.BlockSpec((None, D, eb), lambda i, s: (jnp.minimum(2 * s, nb - 1), 0, 0)),
                  big, big, big, big, row, pl.BlockSpec((1, D), lambda i, s: (0, 0))],
        out_specs=pl.BlockSpec((tm, D), lambda i, s: (i, 0)),
        out_shape=jax.ShapeDtypeStruct((S, D), F32),
        scratch_shapes=[pltpu.VMEM((D, tm), F32), pltpu.VMEM((2, eb, tm), F32),
                        pltpu.VMEM((2, eb, tm), BF16)],
        compiler_params=_cparams(("arbitrary", "arbitrary")),
        name="peer_dense_mix",
    )(h2, u, vt, vt, n, e1, r2, e2, x, g2)


def _final_norm_kernel(x_ref, g_ref, o_ref):
    x = x_ref[...]
    o_ref[...] = x * lax.rsqrt(jnp.mean(x * x, axis=-1, keepdims=True) + RMS_EPS) * g_ref[...]


def _final_norm(x, g):
    S, D = x.shape
    tm = 512
    return pl.pallas_call(
        _final_norm_kernel,
        grid=(S // tm,),
        in_specs=[pl.BlockSpec((tm, D), lambda i: (i, 0)), pl.BlockSpec((1, D), lambda i: (0, 0))],
        out_specs=pl.BlockSpec((tm, D), lambda i: (i, 0)),
        out_shape=jax.ShapeDtypeStruct((S, D), F32),
        compiler_params=_cparams(("arbitrary",)),
        name="final_rmsnorm",
    )(x, g)


def kernel(x, c, ada_w, ada_b, norm1_g, norm2_g, final_g, w_in, fox_fbias, gla_wa2, gla_ba,
           gla_norm_g, w_out, peer_wq, peer_k1, peer_k2, peer_u, peer_v):
    B, S, D = x.shape
    assert B == 1 and D == D_MODEL
    L = ada_w.shape[0]
    xs = x.reshape(S, D)
    mod = _modulation(c, ada_w, ada_b)
    for l in range(L):
        sh1, sc1, g1, sh2, sc2, g2 = [mod[l, :, m * D:(m + 1) * D] for m in range(N_MOD)]
        z, zb = _inproj(xs, norm1_g[l].reshape(1, D), sc1, sh1, _layout_w_in(w_in, l))

        fcol, frow = _forget_cumsum(z, jnp.pad(fox_fbias[l], (0, LANE - FOX_HEADS)).reshape(1, LANE))
        fox = _fox_attention(z, zb, fcol, frow)

        wa2_pad = jnp.pad(gla_wa2[l], ((0, LANE - GLA_GATE_RANK), (0, 0)))
        gla = _gla(z, wa2_pad, gla_ba[l].reshape(1, GLA_W), gla_norm_g[l].reshape(1, HEAD_DIM))

        dsa = _dsa_attention(z, zb, _dsa_mask(z, zb))

        xs, h2 = _outproj(fox, gla, dsa, _cast_layer(w_out, l, "w_out_cast"), xs, g1,
                          norm2_g[l].reshape(1, D), sc2, sh2)

        n, e1, r2, e2 = _peer_scores(h2, _cast_layer(peer_wq, l, "peer_wq_cast"),
                                     peer_k1[l].astype(BF16), peer_k2[l].astype(BF16))
        vt = _transpose_cast(peer_v, l, PEER_EXPERTS, D, 0, 512, 512, "peer_v_transpose",
                             blocked=True)
        xs = _peer_mix(h2, _cast_layer(peer_u, l, "peer_u_cast"), vt, n, e1, r2, e2, xs, g2)
    return _final_norm(xs, final_g.reshape(1, D)).reshape(B, S, D)
```
